```python
import jax, jax.numpy as jnp
from jax import lax
import numpy as np

D_MODEL = 1024
BATCH = 8
SEQ = 2048
DEPTH = 1
DEC_BATCH = 128
DEC_SEQ = 1
PAST_LEN = 16384
PAGE_SIZE = 128

D_MIX = 2 * D_MODEL
D_A = D_MIX // 2
D_B = D_MIX - D_A
CHUNK = 128
HD_A = 128
H_A = D_A // HD_A
H_B = 8
CONV_W = 3
EPS = 1e-5
SPLIT_SIZES = (D_A, D_A, D_A, D_B, D_B, D_B, D_B)
SPLIT_IDX = tuple(int(i) for i in np.cumsum(SPLIT_SIZES)[:-1])
D_IN = sum(SPLIT_SIZES)

kernel_name = "hymba_chunk_gmlp_shortconv_step"


def rmsnorm(x, g):
    xf = x.astype(jnp.float32)
    inv = lax.rsqrt(jnp.mean(xf * xf, axis=-1, keepdims=True) + EPS)
    return (xf * inv * g.astype(jnp.float32)).astype(x.dtype)


def chunk_mix(v, w_s, b_s):
    n, t = v.shape[0], v.shape[1]
    pad = (-t) % CHUNK
    vp = jnp.pad(v, ((0, 0), (0, pad), (0, 0), (0, 0)))
    nc = (t + pad) // CHUNK
    vp = vp.reshape(n, nc, CHUNK, H_A, HD_A)
    mask = jnp.tril(jnp.ones((CHUNK, CHUNK), dtype=w_s.dtype))
    w = w_s * mask[None]
    out = jnp.einsum('hts,ncshd->ncthd', w, vp)
    out = out + jnp.transpose(b_s)[None, None, :, :, None]
    return out.reshape(n, nc * CHUNK, H_A, HD_A)[:, :t]


def mixer_layer(x, conv_state, g_norm, w_in, w_s, b_s, g_v, conv_w, w_out):
    n, t = x.shape[0], x.shape[1]
    h = rmsnorm(x, g_norm)
    proj = jnp.einsum('btd,de->bte', h, w_in)
    u, v, z_a, x_b, gate_b, gate_c, z_b = jnp.split(proj, SPLIT_IDX, axis=-1)
    u = jax.nn.gelu(u, approximate=False)
    v = rmsnorm(jax.nn.gelu(v, approximate=False), g_v)
    mixed = chunk_mix(v.reshape(n, t, H_A, HD_A), w_s, b_s).reshape(n, t, D_A)
    out_a = u * mixed * jax.nn.silu(z_a)
    hb = gate_c * x_b
    hb_ext = jnp.concatenate([conv_state.astype(hb.dtype), hb], axis=1)
    conv = hb_ext[:, 0:t] * conv_w[0]
    for k in range(1, CONV_W):
        conv = conv + hb_ext[:, k:k + t] * conv_w[k]
    out_b = gate_b * conv * jax.nn.silu(z_b)
    mix = jnp.concatenate([out_a, out_b], axis=-1)
    y = x + jnp.einsum('bte,ed->btd', mix, w_out)
    new_conv = hb_ext[:, t:]
    return y, new_conv, v


def setup_inputs(seed: int = 0) -> dict:
    key = jax.random.key(seed)
    ks = jax.random.split(key, 12)
    f32 = jnp.float32
    x_prompt = jax.random.normal(ks[0], (BATCH, SEQ, D_MODEL), f32)
    x_sample = jax.random.normal(ks[1], (DEC_BATCH, DEC_SEQ, D_MODEL), f32)
    state_conv = jax.random.normal(ks[2], (DEPTH, DEC_BATCH, CONV_W - 1, D_B), f32)
    g_norm = 1.0 + 0.02 * jax.random.normal(ks[3], (DEPTH, D_MODEL), f32)
    w_in = jax.random.normal(ks[4], (DEPTH, D_MODEL, D_IN), f32) * D_MODEL ** -0.5
    w_s = jax.random.normal(ks[5], (DEPTH, H_A, CHUNK, CHUNK), f32) * (0.5 * CHUNK ** -0.5)
    b_s = 1.0 + 0.1 * jax.random.normal(ks[6], (DEPTH, H_A, CHUNK), f32)
    g_v = 1.0 + 0.02 * jax.random.normal(ks[7], (DEPTH, D_A), f32)
    conv_w = jax.random.normal(ks[8], (DEPTH, CONV_W, D_B), f32) * CONV_W ** -0.5
    w_out = jax.random.normal(ks[9], (DEPTH, D_MIX, D_MODEL), f32) * D_MIX ** -0.5
    g_final = 1.0 + 0.02 * jax.random.normal(ks[10], (D_MODEL,), f32)
    return {"x_prompt": x_prompt, "x_sample": x_sample, "state_conv": state_conv,
            "g_norm": g_norm, "w_in": w_in, "w_s": w_s, "b_s": b_s, "g_v": g_v,
            "conv_w": conv_w, "w_out": w_out, "g_final": g_final}


def reference(x_prompt, x_sample, state_conv, g_norm, w_in, w_s, b_s, g_v, conv_w, w_out, g_final):
    xp, xs = x_prompt, x_sample
    conv_p_list, conv_s_list, v_s_list = [], [], []
    for l in range(DEPTH):
        zero_state = jnp.zeros((xp.shape[0], CONV_W - 1, D_B), xp.dtype)
        xp, conv_p, _ = mixer_layer(xp, zero_state, g_norm[l], w_in[l], w_s[l], b_s[l],
                                    g_v[l], conv_w[l], w_out[l])
        xs, conv_s, v_s = mixer_layer(xs, state_conv[l], g_norm[l], w_in[l], w_s[l], b_s[l],
                                      g_v[l], conv_w[l], w_out[l])
        conv_p_list.append(conv_p)
        conv_s_list.append(conv_s)
        v_s_list.append(v_s)
    y_prompt = rmsnorm(xp, g_final)
    y_sample = rmsnorm(xs, g_final)
    state_conv_prompt = jnp.stack(conv_p_list, axis=0)
    state_conv_sample = jnp.stack(conv_s_list, axis=0)
    state_v_sample = jnp.stack(v_s_list, axis=0)
    return (y_prompt, y_sample, state_conv_prompt, state_conv_sample, state_v_sample)
```

```python
import functools

import jax
import jax.numpy as jnp
import numpy as np
from jax import lax
from jax.experimental import pallas as pl
from jax.experimental.pallas import tpu as pltpu

D_MODEL = 1024
D_A = 1024
D_B = 1024
D_MIX = D_A + D_B
D_IN = 3 * D_A + 4 * D_B
CHUNK = 128
HD_A = 128
H_A = D_A // HD_A
CONV_W = 3
EPS = 1e-5
SQRT_HALF = float(np.sqrt(0.5))

OFF_U, OFF_V, OFF_ZA, OFF_XB, OFF_GB, OFF_GC, OFF_ZB = (
    0, D_A, 2 * D_A, 3 * D_A, 3 * D_A + D_B, 3 * D_A + 2 * D_B, 3 * D_A + 3 * D_B)

SUBLANES = 8
TILE_M = 256
VMEM_LIMIT_BYTES = 56 * 1024 * 1024


def _rmsnorm(x, g):
    inv = lax.rsqrt(jnp.mean(x * x, axis=-1, keepdims=True) + EPS)
    return x * inv * g


def _gelu(x):
    return 0.5 * x * (1.0 + lax.erf(x * SQRT_HALF))


def _silu(x):
    return x * jax.nn.sigmoid(x)


def _proj(h_bf, w_in_ref, off, width):
    return jnp.dot(h_bf, w_in_ref[:, off:off + width],
                   preferred_element_type=jnp.float32)


def _prompt_kernel(x_ref, gn_ref, w_in_ref, ws_ref, bt_ref, gv_ref, cw_ref,
                   w_out_ref, gf_ref, y_ref, cs_ref, carry_ref, mix_ref):
    t = pl.program_id(1)
    nt = pl.num_programs(1)
    tm = x_ref.shape[0]

    @pl.when(t == 0)
    def _():
        carry_ref[...] = jnp.zeros_like(carry_ref)

    x = x_ref[...]
    h = _rmsnorm(x, gn_ref[...]).astype(jnp.bfloat16)

    v = _rmsnorm(_gelu(_proj(h, w_in_ref, OFF_V, D_A)), gv_ref[...])
    v_bf = v.astype(jnp.bfloat16)
    u = _gelu(_proj(h, w_in_ref, OFF_U, D_A))
    gate_a = _silu(_proj(h, w_in_ref, OFF_ZA, D_A))
    row = lax.broadcasted_iota(jnp.int32, (CHUNK, CHUNK), 0)
    col = lax.broadcasted_iota(jnp.int32, (CHUNK, CHUNK), 1)
    bt = bt_ref[...]
    for hd in range(H_A):
        w_tril = jnp.where(row >= col, ws_ref[hd], 0.0).astype(jnp.bfloat16)
        cs = slice(hd * HD_A, (hd + 1) * HD_A)
        for c in range(tm // CHUNK):
            rs = slice(c * CHUNK, (c + 1) * CHUNK)
            mixed = jnp.dot(w_tril, v_bf[rs, cs],
                            preferred_element_type=jnp.float32) + bt[:, hd:hd + 1]
            mix_ref[rs, cs] = (u[rs, cs] * mixed * gate_a[rs, cs]).astype(jnp.bfloat16)

    hb = _proj(h, w_in_ref, OFF_GC, D_B) * _proj(h, w_in_ref, OFF_XB, D_B)
    carry = carry_ref[...]
    prev1 = carry[SUBLANES - 1:SUBLANES]
    prev2 = carry[SUBLANES - 2:SUBLANES - 1]
    ridx = lax.broadcasted_iota(jnp.int32, (tm, D_B), 0)
    hb_m1 = jnp.where(ridx == 0, prev1, pltpu.roll(hb, 1, 0))
    hb_m2 = jnp.where(ridx == 0, prev2,
                      jnp.where(ridx == 1, prev1, pltpu.roll(hb, 2, 0)))
    cw = cw_ref[...]
    conv = hb_m2 * cw[0:1] + hb_m1 * cw[1:2] + hb * cw[2:3]
    carry_ref[...] = hb[tm - SUBLANES:tm]
    out_b = _proj(h, w_in_ref, OFF_GB, D_B) * conv * _silu(_proj(h, w_in_ref, OFF_ZB, D_B))
    mix_ref[:, D_A:D_MIX] = out_b.astype(jnp.bfloat16)

    @pl.when(t == nt - 1)
    def _():
        cs_ref[...] = hb[tm - (CONV_W - 1):tm]

    y = x + jnp.dot(mix_ref[...], w_out_ref[...], preferred_element_type=jnp.float32)
    y_ref[...] = _rmsnorm(y, gf_ref[...])


def _sample_kernel(x_ref, st_ref, gn_ref, w_in_ref, ws_ref, bt_ref, gv_ref, cw_ref,
                   w_out_ref, gf_ref, y_ref, cs_ref, v_ref, mix_ref):
    x = x_ref[...]
    h = _rmsnorm(x, gn_ref[...]).astype(jnp.bfloat16)

    v = _rmsnorm(_gelu(_proj(h, w_in_ref, OFF_V, D_A)), gv_ref[...])
    v_ref[...] = v
    u = _gelu(_proj(h, w_in_ref, OFF_U, D_A))
    gate_a = _silu(_proj(h, w_in_ref, OFF_ZA, D_A))
    bt = bt_ref[...]
    for hd in range(H_A):
        cs = slice(hd * HD_A, (hd + 1) * HD_A)
        mixed = v[:, cs] * ws_ref[hd, 0:1, 0:1] + bt[0:1, hd:hd + 1]
        mix_ref[:, cs] = (u[:, cs] * mixed * gate_a[:, cs]).astype(jnp.bfloat16)

    hb = _proj(h, w_in_ref, OFF_GC, D_B) * _proj(h, w_in_ref, OFF_XB, D_B)
    s0 = st_ref[:, 0:D_B]
    s1 = st_ref[:, D_B:2 * D_B]
    cw = cw_ref[...]
    conv = s0 * cw[0:1] + s1 * cw[1:2] + hb * cw[2:3]
    out_b = _proj(h, w_in_ref, OFF_GB, D_B) * conv * _silu(_proj(h, w_in_ref, OFF_ZB, D_B))
    mix_ref[:, D_A:D_MIX] = out_b.astype(jnp.bfloat16)
    cs_ref[:, 0:D_B] = s1
    cs_ref[:, D_B:2 * D_B] = hb

    y = x + jnp.dot(mix_ref[...], w_out_ref[...], preferred_element_type=jnp.float32)
    y_ref[...] = _rmsnorm(y, gf_ref[...])


def _resident(shape):
    return pl.BlockSpec(shape, lambda *_: (0,) * len(shape),
                        pipeline_mode=pl.Buffered(1))


def _weight_specs():
    return [
        _resident((1, D_MODEL)),
        _resident((D_MODEL, D_IN)),
        _resident((H_A, CHUNK, CHUNK)),
        _resident((CHUNK, H_A)),
        _resident((1, D_A)),
        _resident((CONV_W, D_B)),
        _resident((D_MIX, D_MODEL)),
        _resident((1, D_MODEL)),
    ]


def _prompt_call(x_prompt, weights):
    batch, seq, _ = x_prompt.shape
    assert seq % TILE_M == 0 and TILE_M % CHUNK == 0
    return pl.pallas_call(
        _prompt_kernel,
        grid=(batch, seq // TILE_M),
        in_specs=[pl.BlockSpec((None, TILE_M, D_MODEL), lambda b, t: (b, t, 0))]
        + _weight_specs(),
        out_specs=[
            pl.BlockSpec((None, TILE_M, D_MODEL), lambda b, t: (b, t, 0)),
            pl.BlockSpec((None, CONV_W - 1, D_B), lambda b, t: (b, 0, 0)),
        ],
        out_shape=[
            jax.ShapeDtypeStruct((batch, seq, D_MODEL), jnp.float32),
            jax.ShapeDtypeStruct((batch, CONV_W - 1, D_B), jnp.float32),
        ],
        scratch_shapes=[
            pltpu.VMEM((SUBLANES, D_B), jnp.float32),
            pltpu.VMEM((TILE_M, D_MIX), jnp.bfloat16),
        ],
        compiler_params=pltpu.CompilerParams(
            dimension_semantics=("arbitrary", "arbitrary"),
            vmem_limit_bytes=VMEM_LIMIT_BYTES),
        name="prompt_layer",
    )(x_prompt, *weights)


def _sample_call(x_s, state, weights):
    n = x_s.shape[0]
    return pl.pallas_call(
        _sample_kernel,
        grid=(1,),
        in_specs=[_resident((n, D_MODEL)), _resident((n, (CONV_W - 1) * D_B))]
        + _weight_specs(),
        out_specs=[
            pl.BlockSpec((n, D_MODEL), lambda i: (0, 0)),
            pl.BlockSpec((n, (CONV_W - 1) * D_B), lambda i: (0, 0)),
            pl.BlockSpec((n, D_A), lambda i: (0, 0)),
        ],
        out_shape=[
            jax.ShapeDtypeStruct((n, D_MODEL), jnp.float32),
            jax.ShapeDtypeStruct((n, (CONV_W - 1) * D_B), jnp.float32),
            jax.ShapeDtypeStruct((n, D_A), jnp.float32),
        ],
        scratch_shapes=[pltpu.VMEM((n, D_MIX), jnp.bfloat16)],
        compiler_params=pltpu.CompilerParams(
            dimension_semantics=("arbitrary",),
            vmem_limit_bytes=VMEM_LIMIT_BYTES),
        name="sample_layer",
    )(x_s, state, *weights)


def kernel(x_prompt, x_sample, state_conv, g_norm, w_in, w_s, b_s, g_v, conv_w, w_out, g_final):
    depth = w_in.shape[0]
    assert depth == 1, "final norm is fused into the single layer"
    n_s, dec_seq, _ = x_sample.shape
    assert dec_seq == 1
    weights = (
        g_norm[0][None, :],
        w_in[0].astype(jnp.bfloat16),
        w_s[0],
        jnp.transpose(b_s[0]),
        g_v[0][None, :],
        conv_w[0],
        w_out[0].astype(jnp.bfloat16),
        g_final[None, :],
    )
    y_p, cs_p = _prompt_call(x_prompt, weights)
    y_s, cs_s, v_s = _sample_call(
        x_sample.reshape(n_s, D_MODEL),
        state_conv[0].reshape(n_s, (CONV_W - 1) * D_B), weights)
    return (
        y_p,
        y_s.reshape(n_s, 1, D_MODEL),
        cs_p[None],
        cs_s.reshape(1, n_s, CONV_W - 1, D_B),
        v_s.reshape(1, n_s, 1, D_A),
    )
```

```python
import functools

import jax
import jax.numpy as jnp
import numpy as np
from jax import lax
from jax.experimental import pallas as pl
from jax.experimental.pallas import tpu as pltpu

D_MODEL = 1024
D_A = 1024
D_B = 1024
D_MIX = D_A + D_B
D_IN = 3 * D_A + 4 * D_B
CHUNK = 128
HD_A = 128
H_A = D_A // HD_A
CONV_W = 3
EPS = 1e-5
SQRT_HALF = float(np.sqrt(0.5))

OFF_U, OFF_V, OFF_ZA, OFF_XB, OFF_GB, OFF_GC, OFF_ZB = (
    0, D_A, 2 * D_A, 3 * D_A, 3 * D_A + D_B, 3 * D_A + 2 * D_B, 3 * D_A + 3 * D_B)

SUBLANES = 8
TILE_M = 512
VMEM_LIMIT_BYTES = 56 * 1024 * 1024


def _rmsnorm(x, g):
    inv = lax.rsqrt(jnp.mean(x * x, axis=-1, keepdims=True) + EPS)
    return x * inv * g


def _gelu(x):
    return 0.5 * x * (1.0 + lax.erf(x * SQRT_HALF))


def _silu(x):
    return x * jax.nn.sigmoid(x)


def _proj(h_bf, w_in_ref, off, width):
    return jnp.dot(h_bf, w_in_ref[:, off:off + width],
                   preferred_element_type=jnp.float32)


def _prompt_kernel(x_ref, gn_ref, w_in_ref, ws_ref, bt_ref, gv_ref, cw_ref,
                   w_out_ref, gf_ref, y_ref, cs_ref, carry_ref, mix_ref):
    t = pl.program_id(1)
    tm = x_ref.shape[0]

    x = x_ref[...]
    h = _rmsnorm(x, gn_ref[...]).astype(jnp.bfloat16)

    v = _rmsnorm(_gelu(_proj(h, w_in_ref, OFF_V, D_A)), gv_ref[...])
    v_bf = v.astype(jnp.bfloat16)

    hb = _proj(h, w_in_ref, OFF_GC, D_B) * _proj(h, w_in_ref, OFF_XB, D_B)
    carry = jnp.where(t == 0, 0.0, carry_ref[...])
    prev1 = carry[SUBLANES - 1:SUBLANES]
    prev2 = carry[SUBLANES - 2:SUBLANES - 1]
    ridx = lax.broadcasted_iota(jnp.int32, (tm, D_B), 0)
    hb_m1 = jnp.where(ridx == 0, prev1, pltpu.roll(hb, 1, 0))
    hb_m2 = jnp.where(ridx == 0, prev2,
                      jnp.where(ridx == 1, prev1, pltpu.roll(hb, 2, 0)))
    cw = cw_ref[...]
    conv = hb_m2 * cw[0:1] + hb_m1 * cw[1:2] + hb * cw[2:3]
    carry_ref[...] = hb[tm - SUBLANES:tm]
    cs_ref[...] = hb[tm - (CONV_W - 1):tm]
    out_b = _proj(h, w_in_ref, OFF_GB, D_B) * conv * _silu(_proj(h, w_in_ref, OFF_ZB, D_B))
    y = x + jnp.dot(out_b.astype(jnp.bfloat16), w_out_ref[D_A:D_MIX, :],
                    preferred_element_type=jnp.float32)

    u = _gelu(_proj(h, w_in_ref, OFF_U, D_A))
    gate_a = _silu(_proj(h, w_in_ref, OFF_ZA, D_A))
    row = lax.broadcasted_iota(jnp.int32, (CHUNK, CHUNK), 0)
    col = lax.broadcasted_iota(jnp.int32, (CHUNK, CHUNK), 1)
    bt = bt_ref[...]
    for hd in range(H_A):
        w_tril = jnp.where(row >= col, ws_ref[hd], 0.0).astype(jnp.bfloat16)
        cs = slice(hd * HD_A, (hd + 1) * HD_A)
        for c in range(tm // CHUNK):
            rs = slice(c * CHUNK, (c + 1) * CHUNK)
            mixed = jnp.dot(w_tril, v_bf[rs, cs],
                            preferred_element_type=jnp.float32) + bt[:, hd:hd + 1]
            mix_ref[rs, cs] = (u[rs, cs] * mixed * gate_a[rs, cs]).astype(jnp.bfloat16)

    y = y + jnp.dot(mix_ref[...], w_out_ref[0:D_A, :], preferred_element_type=jnp.float32)
    y_ref[...] = _rmsnorm(y, gf_ref[...])


def _sample_kernel(x_ref, st_ref, gn_ref, w_in_ref, ws_ref, bt_ref, gv_ref, cw_ref,
                   w_out_ref, gf_ref, y_ref, cs_ref, v_ref, mix_ref):
    x = x_ref[...]
    h = _rmsnorm(x, gn_ref[...]).astype(jnp.bfloat16)

    v = _rmsnorm(_gelu(_proj(h, w_in_ref, OFF_V, D_A)), gv_ref[...])
    v_ref[...] = v
    u = _gelu(_proj(h, w_in_ref, OFF_U, D_A))
    gate_a = _silu(_proj(h, w_in_ref, OFF_ZA, D_A))
    bt = bt_ref[...]
    for hd in range(H_A):
        cs = slice(hd * HD_A, (hd + 1) * HD_A)
        mixed = v[:, cs] * ws_ref[hd, 0:1, 0:1] + bt[0:1, hd:hd + 1]
        mix_ref[:, cs] = (u[:, cs] * mixed * gate_a[:, cs]).astype(jnp.bfloat16)

    hb = _proj(h, w_in_ref, OFF_GC, D_B) * _proj(h, w_in_ref, OFF_XB, D_B)
    s0 = st_ref[:, 0:D_B]
    s1 = st_ref[:, D_B:2 * D_B]
    cw = cw_ref[...]
    conv = s0 * cw[0:1] + s1 * cw[1:2] + hb * cw[2:3]
    out_b = _proj(h, w_in_ref, OFF_GB, D_B) * conv * _silu(_proj(h, w_in_ref, OFF_ZB, D_B))
    mix_ref[:, D_A:D_MIX] = out_b.astype(jnp.bfloat16)
    cs_ref[:, 0:D_B] = s1
    cs_ref[:, D_B:2 * D_B] = hb

    y = x + jnp.dot(mix_ref[...], w_out_ref[...], preferred_element_type=jnp.float32)
    y_ref[...] = _rmsnorm(y, gf_ref[...])


def _resident(shape):
    return pl.BlockSpec(shape, lambda *_: (0,) * len(shape),
                        pipeline_mode=pl.Buffered(1))


def _weight_specs():
    return [
        _resident((1, D_MODEL)),
        _resident((D_MODEL, D_IN)),
        _resident((H_A, CHUNK, CHUNK)),
        _resident((CHUNK, H_A)),
        _resident((1, D_A)),
        _resident((CONV_W, D_B)),
        _resident((D_MIX, D_MODEL)),
        _resident((1, D_MODEL)),
    ]


def _prompt_call(x_prompt, weights):
    batch, seq, _ = x_prompt.shape
    assert seq % TILE_M == 0 and TILE_M % CHUNK == 0
    return pl.pallas_call(
        _prompt_kernel,
        grid=(batch, seq // TILE_M),
        in_specs=[pl.BlockSpec((None, TILE_M, D_MODEL), lambda b, t: (b, t, 0))]
        + _weight_specs(),
        out_specs=[
            pl.BlockSpec((None, TILE_M, D_MODEL), lambda b, t: (b, t, 0)),
            pl.BlockSpec((None, CONV_W - 1, D_B), lambda b, t: (b, 0, 0)),
        ],
        out_shape=[
            jax.ShapeDtypeStruct((batch, seq, D_MODEL), jnp.float32),
            jax.ShapeDtypeStruct((batch, CONV_W - 1, D_B), jnp.float32),
        ],
        scratch_shapes=[
            pltpu.VMEM((SUBLANES, D_B), jnp.float32),
            pltpu.VMEM((TILE_M, D_A), jnp.bfloat16),
        ],
        compiler_params=pltpu.CompilerParams(
            dimension_semantics=("arbitrary", "arbitrary"),
            vmem_limit_bytes=VMEM_LIMIT_BYTES),
        name="prompt_layer",
    )(x_prompt, *weights)


def _sample_call(x_s, state, weights):
    n = x_s.shape[0]
    return pl.pallas_call(
        _sample_kernel,
        grid=(1,),
        in_specs=[_resident((n, D_MODEL)), _resident((n, (CONV_W - 1) * D_B))]
        + _weight_specs(),
        out_specs=[
            pl.BlockSpec((n, D_MODEL), lambda i: (0, 0)),
            pl.BlockSpec((n, (CONV_W - 1) * D_B), lambda i: (0, 0)),
            pl.BlockSpec((n, D_A), lambda i: (0, 0)),
        ],
        out_shape=[
            jax.ShapeDtypeStruct((n, D_MODEL), jnp.float32),
            jax.ShapeDtypeStruct((n, (CONV_W - 1) * D_B), jnp.float32),
            jax.ShapeDtypeStruct((n, D_A), jnp.float32),
        ],
        scratch_shapes=[pltpu.VMEM((n, D_MIX), jnp.bfloat16)],
        compiler_params=pltpu.CompilerParams(
            dimension_semantics=("arbitrary",),
            vmem_limit_bytes=VMEM_LIMIT_BYTES),
        name="sample_layer",
    )(x_s, state, *weights)


def kernel(x_prompt, x_sample, state_conv, g_norm, w_in, w_s, b_s, g_v, conv_w, w_out, g_final):
    depth = w_in.shape[0]
    assert depth == 1, "final norm is fused into the single layer"
    n_s, dec_seq, _ = x_sample.shape
    assert dec_seq == 1
    weights = (
        g_norm[0][None, :],
        w_in[0].astype(jnp.bfloat16),
        w_s[0],
        jnp.transpose(b_s[0]),
        g_v[0][None, :],
        conv_w[0],
        w_out[0].astype(jnp.bfloat16),
        g_final[None, :],
    )
    y_p, cs_p = _prompt_call(x_prompt, weights)
    y_s, cs_s, v_s = _sample_call(
        x_sample.reshape(n_s, D_MODEL),
        state_conv[0].reshape(n_s, (CONV_W - 1) * D_B), weights)
    return (
        y_p,
        y_s.reshape(n_s, 1, D_MODEL),
        cs_p[None],
        cs_s.reshape(1, n_s, CONV_W - 1, D_B),
        v_s.reshape(1, n_s, 1, D_A),
    )
```

```python
import jax
import jax.numpy as jnp
import numpy as np
from jax import lax
from jax.experimental import pallas as pl
from jax.experimental.pallas import tpu as pltpu

D_MODEL = 1024
D_A = 1024
D_B = 1024
D_MIX = D_A + D_B
D_IN = 3 * D_A + 4 * D_B
CHUNK = 128
HD_A = 128
H_A = D_A // HD_A
CONV_W = 3
EPS = 1e-5
SQRT_HALF = float(np.sqrt(0.5))

OFF_U, OFF_V, OFF_ZA, OFF_XB, OFF_GB, OFF_GC, OFF_ZB = (
    0, D_A, 2 * D_A, 3 * D_A, 3 * D_A + D_B, 3 * D_A + 2 * D_B, 3 * D_A + 3 * D_B)

SUBLANES = 8
TILE_M = 512
STAGE_ROWS, STAGE_COLS = 1024, 512
VMEM_LIMIT_BYTES = 58 * 1024 * 1024


def _rmsnorm(x, g):
    inv = lax.rsqrt(jnp.mean(x * x, axis=-1, keepdims=True) + EPS)
    return x * inv * g


def _gelu(x):
    return 0.5 * x * (1.0 + lax.erf(x * SQRT_HALF))


def _silu(x):
    return x * jax.nn.sigmoid(x)


def _proj(h_bf, w_in_ref, off, width):
    return jnp.dot(h_bf, w_in_ref[:, off:off + width],
                   preferred_element_type=jnp.float32)


def _load_weights_bf16(w_in_hbm, w_out_hbm, w_in_bf, w_out_bf, stage, sem):
    def blocks(src, dst):
        rows, cols = src.shape
        return [(src.at[pl.ds(r, STAGE_ROWS), pl.ds(c, STAGE_COLS)],
                 dst.at[pl.ds(r, STAGE_ROWS), pl.ds(c, STAGE_COLS)])
                for r in range(0, rows, STAGE_ROWS) for c in range(0, cols, STAGE_COLS)]

    chunks = blocks(w_in_hbm, w_in_bf) + blocks(w_out_hbm, w_out_bf)

    def copy(k):
        return pltpu.make_async_copy(chunks[k][0], stage.at[k % 2], sem.at[k % 2])

    copy(0).start()
    for k in range(len(chunks)):
        if k + 1 < len(chunks):
            copy(k + 1).start()
        copy(k).wait()
        chunks[k][1][...] = stage[k % 2].astype(jnp.bfloat16)


def _sample_step(xs_ref, st_ref, gn_ref, ws_ref, b_ref, gv_ref, cw_ref, gf_ref,
                 w_in_bf, w_out_bf, ys_ref, css_ref, vs_ref, mix_ref):
    n = xs_ref.shape[0]
    x = xs_ref[...]
    h = _rmsnorm(x, gn_ref[...]).astype(jnp.bfloat16)

    v = _rmsnorm(_gelu(_proj(h, w_in_bf, OFF_V, D_A)), gv_ref[...])
    vs_ref[...] = v
    u = _gelu(_proj(h, w_in_bf, OFF_U, D_A))
    gate_a = _silu(_proj(h, w_in_bf, OFF_ZA, D_A))
    for hd in range(H_A):
        cs = slice(hd * HD_A, (hd + 1) * HD_A)
        mixed = v[:, cs] * ws_ref[hd, 0:1, 0:1] + b_ref[hd:hd + 1, 0:1]
        mix_ref[0:n, cs] = (u[:, cs] * mixed * gate_a[:, cs]).astype(jnp.bfloat16)

    hb = _proj(h, w_in_bf, OFF_GC, D_B) * _proj(h, w_in_bf, OFF_XB, D_B)
    s0 = st_ref[:, 0:D_B]
    s1 = st_ref[:, D_B:2 * D_B]
    cw = cw_ref[...]
    conv = s0 * cw[0:1] + s1 * cw[1:2] + hb * cw[2:3]
    out_b = _proj(h, w_in_bf, OFF_GB, D_B) * conv * _silu(_proj(h, w_in_bf, OFF_ZB, D_B))
    css_ref[:, 0:D_B] = s1
    css_ref[:, D_B:2 * D_B] = hb

    y = x + jnp.dot(out_b.astype(jnp.bfloat16), w_out_bf[D_A:D_MIX, :],
                    preferred_element_type=jnp.float32)
    y = y + jnp.dot(mix_ref[0:n, :], w_out_bf[0:D_A, :], preferred_element_type=jnp.float32)
    ys_ref[...] = _rmsnorm(y, gf_ref[...])


def _prompt_step(t, x_ref, gn_ref, ws_ref, b_ref, gv_ref, cw_ref, gf_ref,
                 w_in_bf, w_out_bf, y_ref, cs_ref, carry_ref, mix_ref):
    tm = x_ref.shape[0]
    x = x_ref[...]
    h = _rmsnorm(x, gn_ref[...]).astype(jnp.bfloat16)

    v = _rmsnorm(_gelu(_proj(h, w_in_bf, OFF_V, D_A)), gv_ref[...])
    v_bf = v.astype(jnp.bfloat16)

    hb = _proj(h, w_in_bf, OFF_GC, D_B) * _proj(h, w_in_bf, OFF_XB, D_B)
    carry = jnp.where(t == 0, 0.0, carry_ref[...])
    prev1 = carry[SUBLANES - 1:SUBLANES]
    prev2 = carry[SUBLANES - 2:SUBLANES - 1]
    ridx = lax.broadcasted_iota(jnp.int32, (tm, D_B), 0)
    hb_m1 = jnp.where(ridx == 0, prev1, pltpu.roll(hb, 1, 0))
    hb_m2 = jnp.where(ridx == 0, prev2,
                      jnp.where(ridx == 1, prev1, pltpu.roll(hb, 2, 0)))
    cw = cw_ref[...]
    conv = hb_m2 * cw[0:1] + hb_m1 * cw[1:2] + hb * cw[2:3]
    carry_ref[...] = hb[tm - SUBLANES:tm]
    cs_ref[...] = hb[tm - (CONV_W - 1):tm]
    out_b = _proj(h, w_in_bf, OFF_GB, D_B) * conv * _silu(_proj(h, w_in_bf, OFF_ZB, D_B))
    y = x + jnp.dot(out_b.astype(jnp.bfloat16), w_out_bf[D_A:D_MIX, :],
                    preferred_element_type=jnp.float32)

    u = _gelu(_proj(h, w_in_bf, OFF_U, D_A))
    gate_a = _silu(_proj(h, w_in_bf, OFF_ZA, D_A))
    row = lax.broadcasted_iota(jnp.int32, (CHUNK, CHUNK), 0)
    col = lax.broadcasted_iota(jnp.int32, (CHUNK, CHUNK), 1)
    bt = jnp.transpose(b_ref[...])
    for hd in range(H_A):
        w_tril = jnp.where(row >= col, ws_ref[hd], 0.0).astype(jnp.bfloat16)
        cs = slice(hd * HD_A, (hd + 1) * HD_A)
        for c in range(tm // CHUNK):
            rs = slice(c * CHUNK, (c + 1) * CHUNK)
            mixed = jnp.dot(w_tril, v_bf[rs, cs],
                            preferred_element_type=jnp.float32) + bt[:, hd:hd + 1]
            mix_ref[rs, cs] = (u[rs, cs] * mixed * gate_a[rs, cs]).astype(jnp.bfloat16)

    y = y + jnp.dot(mix_ref[...], w_out_bf[0:D_A, :], preferred_element_type=jnp.float32)
    y_ref[...] = _rmsnorm(y, gf_ref[...])


def _layer_kernel(tiles_per_seq,
                  x_ref, xs_ref, st_ref, gn_ref, w_in_hbm, ws_ref, b_ref, gv_ref, cw_ref,
                  w_out_hbm, gf_ref,
                  y_ref, cs_ref, ys_ref, css_ref, vs_ref,
                  w_in_bf, w_out_bf, stage, sem, carry_ref, mix_ref):
    s = pl.program_id(0)
    small = (gn_ref, ws_ref, b_ref, gv_ref, cw_ref, gf_ref)

    @pl.when(s == 0)
    def _():
        _load_weights_bf16(w_in_hbm, w_out_hbm, w_in_bf, w_out_bf, stage, sem)
        _sample_step(xs_ref, st_ref, *small, w_in_bf, w_out_bf,
                     ys_ref, css_ref, vs_ref, mix_ref)

    @pl.when(s > 0)
    def _():
        t = lax.rem(s - 1, tiles_per_seq)
        _prompt_step(t, x_ref, *small, w_in_bf, w_out_bf, y_ref, cs_ref, carry_ref, mix_ref)


def _resident(shape):
    return pl.BlockSpec(shape, lambda s: (0,) * len(shape), pipeline_mode=pl.Buffered(1))


def kernel(x_prompt, x_sample, state_conv, g_norm, w_in, w_s, b_s, g_v, conv_w, w_out, g_final):
    depth = w_in.shape[0]
    assert depth == 1, "final norm is fused into the single layer"
    batch, seq, _ = x_prompt.shape
    n_s, dec_seq, _ = x_sample.shape
    assert dec_seq == 1 and n_s <= TILE_M
    assert seq % TILE_M == 0 and TILE_M % CHUNK == 0
    assert D_MODEL % STAGE_ROWS == 0 and D_MIX % STAGE_ROWS == 0
    assert D_IN % STAGE_COLS == 0 and D_MODEL % STAGE_COLS == 0
    tiles_per_seq = seq // TILE_M
    n_tiles = batch * tiles_per_seq

    def tile_index(s):
        i = jnp.maximum(s - 1, 0)
        return i // tiles_per_seq, i % tiles_per_seq

    def x_map(s):
        b, t = tile_index(s)
        return b, t, 0

    def cs_map(s):
        b, _ = tile_index(s)
        return b, 0, 0

    whole = lambda *shape: pl.BlockSpec(shape, lambda s: (0,) * len(shape))
    y_p, cs_p, y_s, cs_s, v_s = pl.pallas_call(
        lambda *refs: _layer_kernel(tiles_per_seq, *refs),
        grid=(n_tiles + 1,),
        in_specs=[
            pl.BlockSpec((None, TILE_M, D_MODEL), x_map),
            _resident((n_s, D_MODEL)),
            _resident((n_s, (CONV_W - 1) * D_B)),
            _resident((1, D_MODEL)),
            pl.BlockSpec(memory_space=pl.ANY),
            _resident((H_A, CHUNK, CHUNK)),
            _resident((H_A, CHUNK)),
            _resident((1, D_A)),
            _resident((CONV_W, D_B)),
            pl.BlockSpec(memory_space=pl.ANY),
            _resident((1, D_MODEL)),
        ],
        out_specs=[
            pl.BlockSpec((None, TILE_M, D_MODEL), x_map),
            pl.BlockSpec((None, CONV_W - 1, D_B), cs_map),
            whole(n_s, D_MODEL),
            whole(n_s, (CONV_W - 1) * D_B),
            whole(n_s, D_A),
        ],
        out_shape=[
            jax.ShapeDtypeStruct((batch, seq, D_MODEL), jnp.float32),
            jax.ShapeDtypeStruct((batch, CONV_W - 1, D_B), jnp.float32),
            jax.ShapeDtypeStruct((n_s, D_MODEL), jnp.float32),
            jax.ShapeDtypeStruct((n_s, (CONV_W - 1) * D_B), jnp.float32),
            jax.ShapeDtypeStruct((n_s, D_A), jnp.float32),
        ],
        scratch_shapes=[
            pltpu.VMEM((D_MODEL, D_IN), jnp.bfloat16),
            pltpu.VMEM((D_MIX, D_MODEL), jnp.bfloat16),
            pltpu.VMEM((2, STAGE_ROWS, STAGE_COLS), jnp.float32),
            pltpu.SemaphoreType.DMA((2,)),
            pltpu.VMEM((SUBLANES, D_B), jnp.float32),
            pltpu.VMEM((TILE_M, D_A), jnp.bfloat16),
        ],
        compiler_params=pltpu.CompilerParams(
            dimension_semantics=("arbitrary",),
            vmem_limit_bytes=VMEM_LIMIT_BYTES),
        name="hybrid_layer",
    )(x_prompt, x_sample.reshape(n_s, D_MODEL),
      state_conv[0].reshape(n_s, (CONV_W - 1) * D_B),
      g_norm, w_in[0], w_s[0], b_s[0], g_v, conv_w[0], w_out[0], g_final[None, :])
    return (
        y_p,
        y_s.reshape(n_s, 1, D_MODEL),
        cs_p[None],
        cs_s.reshape(1, n_s, CONV_W - 1, D_B),
        v_s.reshape(1, n_s, 1, D_A),
    )
```

```python
import jax
import jax.numpy as jnp
import numpy as np
from jax import lax
from jax.experimental import pallas as pl
from jax.experimental.pallas import tpu as pltpu

D_MODEL = 1024
D_A = 1024
D_B = 1024
D_MIX = D_A + D_B
D_IN = 3 * D_A + 4 * D_B
CHUNK = 128
HD_A = 128
H_A = D_A // HD_A
CONV_W = 3
EPS = 1e-5
SQRT_HALF = float(np.sqrt(0.5))

OFF_U, OFF_V, OFF_ZA, OFF_XB, OFF_GB, OFF_GC, OFF_ZB = (
    0, D_A, 2 * D_A, 3 * D_A, 3 * D_A + D_B, 3 * D_A + 2 * D_B, 3 * D_A + 3 * D_B)

SUBLANES = 8
TILE_M = 512
STAGE_ROWS, STAGE_COLS = 1024, 512
VMEM_LIMIT_BYTES = 58 * 1024 * 1024


def _rmsnorm(x, g):
    inv = lax.rsqrt(jnp.mean(x * x, axis=-1, keepdims=True) + EPS)
    return x * inv * g


def _gelu(x):
    return 0.5 * x * (1.0 + lax.erf(x * SQRT_HALF))


def _silu(x):
    return x * jax.nn.sigmoid(x)


def _proj(h_bf, w_in_ref, off, width):
    return jnp.dot(h_bf, w_in_ref[:, off:off + width],
                   preferred_element_type=jnp.float32)


def _load_weights_bf16(w_in_hbm, w_out_hbm, w_in_bf, w_out_bf, stage, sem):
    def blocks(src, dst):
        rows, cols = src.shape
        return [(src.at[pl.ds(r, STAGE_ROWS), pl.ds(c, STAGE_COLS)],
                 dst.at[pl.ds(r, STAGE_ROWS), pl.ds(c, STAGE_COLS)])
                for r in range(0, rows, STAGE_ROWS) for c in range(0, cols, STAGE_COLS)]

    chunks = blocks(w_in_hbm, w_in_bf) + blocks(w_out_hbm, w_out_bf)

    def copy(k):
        return pltpu.make_async_copy(chunks[k][0], stage.at[k % 2], sem.at[k % 2])

    copy(0).start()
    for k in range(len(chunks)):
        if k + 1 < len(chunks):
            copy(k + 1).start()
        copy(k).wait()
        chunks[k][1][...] = stage[k % 2].astype(jnp.bfloat16)


def _sample_step(xs_ref, st_ref, gn_ref, ws_ref, b_ref, gv_ref, cw_ref, gf_ref,
                 w_in_bf, w_out_bf, ys_ref, css_ref, vs_ref, mix_ref):
    n = xs_ref.shape[0]
    x = xs_ref[...]
    h = _rmsnorm(x, gn_ref[...]).astype(jnp.bfloat16)

    v = _rmsnorm(_gelu(_proj(h, w_in_bf, OFF_V, D_A)), gv_ref[...])
    vs_ref[...] = v
    u = _gelu(_proj(h, w_in_bf, OFF_U, D_A))
    gate_a = _silu(_proj(h, w_in_bf, OFF_ZA, D_A))
    for hd in range(H_A):
        cs = slice(hd * HD_A, (hd + 1) * HD_A)
        mixed = v[:, cs] * ws_ref[hd, 0:1, 0:1] + b_ref[hd:hd + 1, 0:1]
        mix_ref[0:n, cs] = (u[:, cs] * mixed * gate_a[:, cs]).astype(jnp.bfloat16)

    hb = _proj(h, w_in_bf, OFF_GC, D_B) * _proj(h, w_in_bf, OFF_XB, D_B)
    s0 = st_ref[:, 0:D_B]
    s1 = st_ref[:, D_B:2 * D_B]
    cw = cw_ref[...]
    conv = s0 * cw[0:1] + s1 * cw[1:2] + hb * cw[2:3]
    out_b = _proj(h, w_in_bf, OFF_GB, D_B) * conv * _silu(_proj(h, w_in_bf, OFF_ZB, D_B))
    css_ref[:, 0:D_B] = s1
    css_ref[:, D_B:2 * D_B] = hb

    y = x + jnp.dot(out_b.astype(jnp.bfloat16), w_out_bf[D_A:D_MIX, :],
                    preferred_element_type=jnp.float32)
    y = y + jnp.dot(mix_ref[0:n, :], w_out_bf[0:D_A, :], preferred_element_type=jnp.float32)
    ys_ref[...] = _rmsnorm(y, gf_ref[...])


def _prompt_step(t, x_ref, gn_ref, ws_ref, b_ref, gv_ref, cw_ref, gf_ref,
                 w_in_bf, w_out_bf, y_ref, cs_ref, carry_ref, mix_ref):
    tm = x_ref.shape[0]
    x = x_ref[...]
    h = _rmsnorm(x, gn_ref[...]).astype(jnp.bfloat16)

    proj = jnp.dot(h, w_in_bf[...], preferred_element_type=jnp.float32)

    def _proj(h_bf, w_ref, off, width):
        return proj[:, off:off + width]

    v = _rmsnorm(_gelu(_proj(h, w_in_bf, OFF_V, D_A)), gv_ref[...])
    v_bf = v.astype(jnp.bfloat16)

    hb = _proj(h, w_in_bf, OFF_GC, D_B) * _proj(h, w_in_bf, OFF_XB, D_B)
    carry = jnp.where(t == 0, 0.0, carry_ref[...])
    prev1 = carry[SUBLANES - 1:SUBLANES]
    prev2 = carry[SUBLANES - 2:SUBLANES - 1]
    ridx = lax.broadcasted_iota(jnp.int32, (tm, D_B), 0)
    hb_m1 = jnp.where(ridx == 0, prev1, pltpu.roll(hb, 1, 0))
    hb_m2 = jnp.where(ridx == 0, prev2,
                      jnp.where(ridx == 1, prev1, pltpu.roll(hb, 2, 0)))
    cw = cw_ref[...]
    conv = hb_m2 * cw[0:1] + hb_m1 * cw[1:2] + hb * cw[2:3]
    carry_ref[...] = hb[tm - SUBLANES:tm]
    cs_ref[...] = hb[tm - (CONV_W - 1):tm]
    out_b = _proj(h, w_in_bf, OFF_GB, D_B) * conv * _silu(_proj(h, w_in_bf, OFF_ZB, D_B))
    y = x + jnp.dot(out_b.astype(jnp.bfloat16), w_out_bf[D_A:D_MIX, :],
                    preferred_element_type=jnp.float32)

    u = _gelu(_proj(h, w_in_bf, OFF_U, D_A))
    gate_a = _silu(_proj(h, w_in_bf, OFF_ZA, D_A))
    row = lax.broadcasted_iota(jnp.int32, (CHUNK, CHUNK), 0)
    col = lax.broadcasted_iota(jnp.int32, (CHUNK, CHUNK), 1)
    bt = jnp.transpose(b_ref[...])
    for hd in range(H_A):
        w_tril = jnp.where(row >= col, ws_ref[hd], 0.0).astype(jnp.bfloat16)
        cs = slice(hd * HD_A, (hd + 1) * HD_A)
        for c in range(tm // CHUNK):
            rs = slice(c * CHUNK, (c + 1) * CHUNK)
            mixed = jnp.dot(w_tril, v_bf[rs, cs],
                            preferred_element_type=jnp.float32) + bt[:, hd:hd + 1]
            mix_ref[rs, cs] = (u[rs, cs] * mixed * gate_a[rs, cs]).astype(jnp.bfloat16)

    y = y + jnp.dot(mix_ref[...], w_out_bf[0:D_A, :], preferred_element_type=jnp.float32)
    y_ref[...] = _rmsnorm(y, gf_ref[...])


def _layer_kernel(tiles_per_seq,
                  x_ref, xs_ref, st_ref, gn_ref, w_in_hbm, ws_ref, b_ref, gv_ref, cw_ref,
                  w_out_hbm, gf_ref,
                  y_ref, cs_ref, ys_ref, css_ref, vs_ref,
                  w_in_bf, w_out_bf, stage, sem, carry_ref, mix_ref):
    s = pl.program_id(0)
    small = (gn_ref, ws_ref, b_ref, gv_ref, cw_ref, gf_ref)

    @pl.when(s == 0)
    def _():
        _load_weights_bf16(w_in_hbm, w_out_hbm, w_in_bf, w_out_bf, stage, sem)
        _sample_step(xs_ref, st_ref, *small, w_in_bf, w_out_bf,
                     ys_ref, css_ref, vs_ref, mix_ref)

    @pl.when(s > 0)
    def _():
        t = lax.rem(s - 1, tiles_per_seq)
        _prompt_step(t, x_ref, *small, w_in_bf, w_out_bf, y_ref, cs_ref, carry_ref, mix_ref)


def _resident(shape):
    return pl.BlockSpec(shape, lambda s: (0,) * len(shape), pipeline_mode=pl.Buffered(1))


def kernel(x_prompt, x_sample, state_conv, g_norm, w_in, w_s, b_s, g_v, conv_w, w_out, g_final):
    depth = w_in.shape[0]
    assert depth == 1, "final norm is fused into the single layer"
    batch, seq, _ = x_prompt.shape
    n_s, dec_seq, _ = x_sample.shape
    assert dec_seq == 1 and n_s <= TILE_M
    assert seq % TILE_M == 0 and TILE_M % CHUNK == 0
    assert D_MODEL % STAGE_ROWS == 0 and D_MIX % STAGE_ROWS == 0
    assert D_IN % STAGE_COLS == 0 and D_MODEL % STAGE_COLS == 0
    tiles_per_seq = seq // TILE_M
    n_tiles = batch * tiles_per_seq

    def tile_index(s):
        i = jnp.maximum(s - 1, 0)
        return i // tiles_per_seq, i % tiles_per_seq

    def x_map(s):
        b, t = tile_index(s)
        return b, t, 0

    def cs_map(s):
        b, _ = tile_index(s)
        return b, 0, 0

    whole = lambda *shape: pl.BlockSpec(shape, lambda s: (0,) * len(shape))
    y_p, cs_p, y_s, cs_s, v_s = pl.pallas_call(
        lambda *refs: _layer_kernel(tiles_per_seq, *refs),
        grid=(n_tiles + 1,),
        in_specs=[
            pl.BlockSpec((None, TILE_M, D_MODEL), x_map),
            _resident((n_s, D_MODEL)),
            _resident((n_s, (CONV_W - 1) * D_B)),
            _resident((1, D_MODEL)),
            pl.BlockSpec(memory_space=pl.ANY),
            _resident((H_A, CHUNK, CHUNK)),
            _resident((H_A, CHUNK)),
            _resident((1, D_A)),
            _resident((CONV_W, D_B)),
            pl.BlockSpec(memory_space=pl.ANY),
            _resident((1, D_MODEL)),
        ],
        out_specs=[
            pl.BlockSpec((None, TILE_M, D_MODEL), x_map),
            pl.BlockSpec((None, CONV_W - 1, D_B), cs_map),
            whole(n_s, D_MODEL),
            whole(n_s, (CONV_W - 1) * D_B),
            whole(n_s, D_A),
        ],
        out_shape=[
            jax.ShapeDtypeStruct((batch, seq, D_MODEL), jnp.float32),
            jax.ShapeDtypeStruct((batch, CONV_W - 1, D_B), jnp.float32),
            jax.ShapeDtypeStruct((n_s, D_MODEL), jnp.float32),
            jax.ShapeDtypeStruct((n_s, (CONV_W - 1) * D_B), jnp.float32),
            jax.ShapeDtypeStruct((n_s, D_A), jnp.float32),
        ],
        scratch_shapes=[
            pltpu.VMEM((D_MODEL, D_IN), jnp.bfloat16),
            pltpu.VMEM((D_MIX, D_MODEL), jnp.bfloat16),
            pltpu.VMEM((2, STAGE_ROWS, STAGE_COLS), jnp.float32),
            pltpu.SemaphoreType.DMA((2,)),
            pltpu.VMEM((SUBLANES, D_B), jnp.float32),
            pltpu.VMEM((TILE_M, D_A), jnp.bfloat16),
        ],
        compiler_params=pltpu.CompilerParams(
            dimension_semantics=("arbitrary",),
            vmem_limit_bytes=VMEM_LIMIT_BYTES),
        name="hybrid_layer",
    )(x_prompt, x_sample.reshape(n_s, D_MODEL),
      state_conv[0].reshape(n_s, (CONV_W - 1) * D_B),
      g_norm, w_in[0], w_s[0], b_s[0], g_v, conv_w[0], w_out[0], g_final[None, :])
    return (
        y_p,
        y_s.reshape(n_s, 1, D_MODEL),
        cs_p[None],
        cs_s.reshape(1, n_s, CONV_W - 1, D_B),
        v_s.reshape(1, n_s, 1, D_A),
    )
```

```python
import jax
import jax.numpy as jnp
import numpy as np
from jax import lax
from jax.experimental import pallas as pl
from jax.experimental.pallas import tpu as pltpu

D_MODEL = 1024
D_A = 1024
D_B = 1024
D_MIX = D_A + D_B
D_IN = 3 * D_A + 4 * D_B
CHUNK = 128
HD_A = 128
H_A = D_A // HD_A
CONV_W = 3
EPS = 1e-5
SQRT_HALF = float(np.sqrt(0.5))

OFF_U, OFF_V, OFF_ZA, OFF_XB, OFF_GB, OFF_GC, OFF_ZB = (
    0, D_A, 2 * D_A, 3 * D_A, 3 * D_A + D_B, 3 * D_A + 2 * D_B, 3 * D_A + 3 * D_B)

SUBLANES = 8
TILE_M = 512
STAGE_ROWS, STAGE_COLS = 1024, 512
VMEM_LIMIT_BYTES = 58 * 1024 * 1024


def _rmsnorm(x, g):
    inv = lax.rsqrt(jnp.mean(x * x, axis=-1, keepdims=True) + EPS)
    return x * inv * g


def _gelu(x):
    return 0.5 * x * (1.0 + lax.erf(x * SQRT_HALF))


def _silu(x):
    return x * jax.nn.sigmoid(x)


def _in_proj(h_bf, w_in_bf):
    proj = jnp.dot(h_bf, w_in_bf[...], preferred_element_type=jnp.float32)
    return lambda off, width: proj[:, off:off + width]


def _load_weights_bf16(w_in_hbm, w_out_hbm, w_in_bf, w_out_bf, stage, sem):
    def blocks(src, dst):
        rows, cols = src.shape
        return [(src.at[pl.ds(r, STAGE_ROWS), pl.ds(c, STAGE_COLS)],
                 dst.at[pl.ds(r, STAGE_ROWS), pl.ds(c, STAGE_COLS)])
                for r in range(0, rows, STAGE_ROWS) for c in range(0, cols, STAGE_COLS)]

    chunks = blocks(w_in_hbm, w_in_bf) + blocks(w_out_hbm, w_out_bf)

    def copy(k):
        return pltpu.make_async_copy(chunks[k][0], stage.at[k % 2], sem.at[k % 2])

    copy(0).start()
    for k in range(len(chunks)):
        if k + 1 < len(chunks):
            copy(k + 1).start()
        copy(k).wait()
        chunks[k][1][...] = stage[k % 2].astype(jnp.bfloat16)


def _sample_step(xs_ref, st_ref, gn_ref, ws_ref, b_ref, gv_ref, cw_ref, gf_ref,
                 w_in_bf, w_out_bf, ys_ref, css_ref, vs_ref, mix_ref):
    n = xs_ref.shape[0]
    x = xs_ref[:, 0, :]
    h = _rmsnorm(x, gn_ref[...]).astype(jnp.bfloat16)
    grp = _in_proj(h, w_in_bf)

    v = _rmsnorm(_gelu(grp(OFF_V, D_A)), gv_ref[...])
    vs_ref[:, 0, :] = v
    u = _gelu(grp(OFF_U, D_A))
    gate_a = _silu(grp(OFF_ZA, D_A))
    for hd in range(H_A):
        cs = slice(hd * HD_A, (hd + 1) * HD_A)
        mixed = v[:, cs] * ws_ref[hd, 0:1, 0:1] + b_ref[hd:hd + 1, 0:1]
        mix_ref[0:n, cs] = (u[:, cs] * mixed * gate_a[:, cs]).astype(jnp.bfloat16)

    hb = grp(OFF_GC, D_B) * grp(OFF_XB, D_B)
    s0 = st_ref[:, 0, :]
    s1 = st_ref[:, 1, :]
    cw = cw_ref[...]
    conv = s0 * cw[0:1] + s1 * cw[1:2] + hb * cw[2:3]
    out_b = grp(OFF_GB, D_B) * conv * _silu(grp(OFF_ZB, D_B))
    css_ref[:, 0, :] = s1
    css_ref[:, 1, :] = hb

    y = x + jnp.dot(out_b.astype(jnp.bfloat16), w_out_bf[D_A:D_MIX, :],
                    preferred_element_type=jnp.float32)
    y = y + jnp.dot(mix_ref[0:n, :], w_out_bf[0:D_A, :], preferred_element_type=jnp.float32)
    ys_ref[:, 0, :] = _rmsnorm(y, gf_ref[...])


def _prompt_step(t, x_ref, gn_ref, ws_ref, b_ref, gv_ref, cw_ref, gf_ref,
                 w_in_bf, w_out_bf, y_ref, cs_ref, carry_ref, mix_ref):
    tm = x_ref.shape[0]
    x = x_ref[...]
    h = _rmsnorm(x, gn_ref[...]).astype(jnp.bfloat16)
    grp = _in_proj(h, w_in_bf)

    v = _rmsnorm(_gelu(grp(OFF_V, D_A)), gv_ref[...])
    v_bf = v.astype(jnp.bfloat16)

    hb = grp(OFF_GC, D_B) * grp(OFF_XB, D_B)
    carry = jnp.where(t == 0, 0.0, carry_ref[...])
    prev1 = carry[SUBLANES - 1:SUBLANES]
    prev2 = carry[SUBLANES - 2:SUBLANES - 1]
    ridx = lax.broadcasted_iota(jnp.int32, (tm, D_B), 0)
    hb_m1 = jnp.where(ridx == 0, prev1, pltpu.roll(hb, 1, 0))
    hb_m2 = jnp.where(ridx == 0, prev2,
                      jnp.where(ridx == 1, prev1, pltpu.roll(hb, 2, 0)))
    cw = cw_ref[...]
    conv = hb_m2 * cw[0:1] + hb_m1 * cw[1:2] + hb * cw[2:3]
    carry_ref[...] = hb[tm - SUBLANES:tm]
    cs_ref[...] = hb[tm - (CONV_W - 1):tm]
    out_b = grp(OFF_GB, D_B) * conv * _silu(grp(OFF_ZB, D_B))
    y = x + jnp.dot(out_b.astype(jnp.bfloat16), w_out_bf[D_A:D_MIX, :],
                    preferred_element_type=jnp.float32)

    u = _gelu(grp(OFF_U, D_A))
    gate_a = _silu(grp(OFF_ZA, D_A))
    row = lax.broadcasted_iota(jnp.int32, (CHUNK, CHUNK), 0)
    col = lax.broadcasted_iota(jnp.int32, (CHUNK, CHUNK), 1)
    bt = jnp.transpose(b_ref[...])
    for hd in range(H_A):
        w_tril = jnp.where(row >= col, ws_ref[hd], 0.0).astype(jnp.bfloat16)
        cs = slice(hd * HD_A, (hd + 1) * HD_A)
        for c in range(tm // CHUNK):
            rs = slice(c * CHUNK, (c + 1) * CHUNK)
            mixed = jnp.dot(w_tril, v_bf[rs, cs],
                            preferred_element_type=jnp.float32) + bt[:, hd:hd + 1]
            mix_ref[rs, cs] = (u[rs, cs] * mixed * gate_a[rs, cs]).astype(jnp.bfloat16)

    y = y + jnp.dot(mix_ref[...], w_out_bf[0:D_A, :], preferred_element_type=jnp.float32)
    y_ref[...] = _rmsnorm(y, gf_ref[...])


def _layer_kernel(tiles_per_seq,
                  x_ref, xs_ref, st_ref, gn_ref, w_in_hbm, ws_ref, b_ref, gv_ref, cw_ref,
                  w_out_hbm, gf_ref,
                  y_ref, cs_ref, ys_ref, css_ref, vs_ref,
                  w_in_bf, w_out_bf, stage, sem, carry_ref, mix_ref):
    s = pl.program_id(0)
    small = (gn_ref, ws_ref, b_ref, gv_ref, cw_ref, gf_ref)

    @pl.when(s == 0)
    def _():
        _load_weights_bf16(w_in_hbm, w_out_hbm, w_in_bf, w_out_bf, stage, sem)
        _sample_step(xs_ref, st_ref, *small, w_in_bf, w_out_bf,
                     ys_ref, css_ref, vs_ref, mix_ref)

    @pl.when(s > 0)
    def _():
        t = lax.rem(s - 1, tiles_per_seq)
        _prompt_step(t, x_ref, *small, w_in_bf, w_out_bf, y_ref, cs_ref, carry_ref, mix_ref)


def _resident(shape, squeeze_leading=False):
    block = ((None,) + tuple(shape[1:])) if squeeze_leading else tuple(shape)
    return pl.BlockSpec(block, lambda s: (0,) * len(shape), pipeline_mode=pl.Buffered(1))


def _whole_out(shape, squeeze_leading=False):
    block = ((None,) + tuple(shape[1:])) if squeeze_leading else tuple(shape)
    return pl.BlockSpec(block, lambda s: (0,) * len(shape))


def kernel(x_prompt, x_sample, state_conv, g_norm, w_in, w_s, b_s, g_v, conv_w, w_out, g_final):
    depth = w_in.shape[0]
    assert depth == 1, "final norm is fused into the single layer"
    batch, seq, _ = x_prompt.shape
    n_s, dec_seq, _ = x_sample.shape
    assert dec_seq == 1 and n_s <= TILE_M
    assert seq % TILE_M == 0 and TILE_M % CHUNK == 0
    assert D_MODEL % STAGE_ROWS == 0 and D_MIX % STAGE_ROWS == 0
    assert D_IN % STAGE_COLS == 0 and D_MODEL % STAGE_COLS == 0
    tiles_per_seq = seq // TILE_M
    n_tiles = batch * tiles_per_seq

    def tile_index(s):
        i = jnp.maximum(s - 1, 0)
        return i // tiles_per_seq, i % tiles_per_seq

    def x_map(s):
        b, t = tile_index(s)
        return b, t, 0

    def cs_map(s):
        b, _ = tile_index(s)
        return b, 0, 0

    cs_s_shape = (depth, n_s, CONV_W - 1, D_B)
    v_s_shape = (depth, n_s, 1, D_A)
    y_p, cs_p, y_s, cs_s, v_s = pl.pallas_call(
        lambda *refs: _layer_kernel(tiles_per_seq, *refs),
        grid=(n_tiles + 1,),
        in_specs=[
            pl.BlockSpec((None, TILE_M, D_MODEL), x_map),
            _resident(x_sample.shape),
            _resident(state_conv.shape, squeeze_leading=True),
            _resident((1, D_MODEL)),
            pl.BlockSpec(memory_space=pl.ANY),
            _resident(w_s.shape, squeeze_leading=True),
            _resident(b_s.shape, squeeze_leading=True),
            _resident((1, D_A)),
            _resident(conv_w.shape, squeeze_leading=True),
            pl.BlockSpec(memory_space=pl.ANY),
            _resident((1, D_MODEL)),
        ],
        out_specs=[
            pl.BlockSpec((None, TILE_M, D_MODEL), x_map),
            pl.BlockSpec((None, CONV_W - 1, D_B), cs_map),
            _whole_out(x_sample.shape),
            _whole_out(cs_s_shape, squeeze_leading=True),
            _whole_out(v_s_shape, squeeze_leading=True),
        ],
        out_shape=[
            jax.ShapeDtypeStruct((batch, seq, D_MODEL), jnp.float32),
            jax.ShapeDtypeStruct((batch, CONV_W - 1, D_B), jnp.float32),
            jax.ShapeDtypeStruct(x_sample.shape, jnp.float32),
            jax.ShapeDtypeStruct(cs_s_shape, jnp.float32),
            jax.ShapeDtypeStruct(v_s_shape, jnp.float32),
        ],
        scratch_shapes=[
            pltpu.VMEM((D_MODEL, D_IN), jnp.bfloat16),
            pltpu.VMEM((D_MIX, D_MODEL), jnp.bfloat16),
            pltpu.VMEM((2, STAGE_ROWS, STAGE_COLS), jnp.float32),
            pltpu.SemaphoreType.DMA((2,)),
            pltpu.VMEM((SUBLANES, D_B), jnp.float32),
            pltpu.VMEM((TILE_M, D_A), jnp.bfloat16),
        ],
        compiler_params=pltpu.CompilerParams(
            dimension_semantics=("arbitrary",),
            vmem_limit_bytes=VMEM_LIMIT_BYTES),
        name="hybrid_layer",
    )(x_prompt, x_sample, state_conv, g_norm, w_in[0], w_s, b_s, g_v, conv_w, w_out[0],
      g_final[None, :])
    return y_p, y_s, cs_p[None], cs_s, v_s
```

```python
import jax
import jax.numpy as jnp
import numpy as np
from jax import lax
from jax.experimental import pallas as pl
from jax.experimental.pallas import tpu as pltpu

D_MODEL = 1024
D_A = 1024
D_B = 1024
D_MIX = D_A + D_B
D_IN = 3 * D_A + 4 * D_B
CHUNK = 128
HD_A = 128
H_A = D_A // HD_A
CONV_W = 3
EPS = 1e-5
SQRT_HALF = float(np.sqrt(0.5))

OFF_U, OFF_V, OFF_ZA, OFF_XB, OFF_GB, OFF_GC, OFF_ZB = (
    0, D_A, 2 * D_A, 3 * D_A, 3 * D_A + D_B, 3 * D_A + 2 * D_B, 3 * D_A + 3 * D_B)

SUBLANES = 8
LANES = 128
TILE_M = 512
STAGE_ROWS, STAGE_COLS = 1024, 512
VMEM_LIMIT_BYTES = 58 * 1024 * 1024

N_BLK = D_A // LANES
V_BASE, HB_BASE, G2_BASE, UA_BASE = 0, N_BLK, 3 * N_BLK, 5 * N_BLK


def _dst_block(src_block):
    group, j = divmod(src_block, N_BLK)
    base, second = {OFF_V: (V_BASE, None), OFF_XB: (HB_BASE, 0), OFF_GC: (HB_BASE, 1),
                    OFF_GB: (G2_BASE, 0), OFF_ZB: (G2_BASE, 1),
                    OFF_U: (UA_BASE, 0), OFF_ZA: (UA_BASE, 1)}[group * D_A]
    return base + j if second is None else base + 2 * j + second


def _cols(base, n_blocks):
    return slice(base * LANES, (base + n_blocks) * LANES)


def _pair(p, j):
    return (p[:, (2 * j) * LANES:(2 * j + 1) * LANES],
            p[:, (2 * j + 1) * LANES:(2 * j + 2) * LANES])


def _dot(a, b):
    return jnp.dot(a, b, preferred_element_type=jnp.float32)


def _rmsnorm(x, g):
    inv = lax.rsqrt(jnp.mean(x * x, axis=-1, keepdims=True) + EPS)
    return x * inv * g


def _gelu(x):
    return 0.5 * x * (1.0 + lax.erf(x * SQRT_HALF))


def _silu(x):
    return x * jax.nn.sigmoid(x)


def _load_weights_bf16(w_in_hbm, w_out_hbm, w_in_bf, w_out_bf, stage, sem):
    def blocks(src):
        rows, cols = src.shape
        return [(src, r, c) for r in range(0, rows, STAGE_ROWS)
                for c in range(0, cols, STAGE_COLS)]

    chunks = blocks(w_in_hbm) + blocks(w_out_hbm)

    def copy(k):
        src, r, c = chunks[k]
        return pltpu.make_async_copy(src.at[pl.ds(r, STAGE_ROWS), pl.ds(c, STAGE_COLS)],
                                     stage.at[k % 2], sem.at[k % 2])

    copy(0).start()
    for k in range(len(chunks)):
        if k + 1 < len(chunks):
            copy(k + 1).start()
        copy(k).wait()
        src, r, c = chunks[k]
        if src is w_in_hbm:
            for i in range(STAGE_COLS // LANES):
                dst = _dst_block(c // LANES + i)
                w_in_bf[:, dst * LANES:(dst + 1) * LANES] = (
                    stage[k % 2, :, i * LANES:(i + 1) * LANES].astype(jnp.bfloat16))
        else:
            w_out_bf[r:r + STAGE_ROWS, c:c + STAGE_COLS] = stage[k % 2].astype(jnp.bfloat16)


def _sample_step(xs_ref, st_ref, gn_ref, ws_ref, b_ref, gv_ref, cw_ref, gf_ref,
                 w_in_bf, w_out_bf, ys_ref, css_ref, vs_ref):
    x = xs_ref[:, 0, :]
    h = _rmsnorm(x, gn_ref[...]).astype(jnp.bfloat16)

    v = _rmsnorm(_gelu(_dot(h, w_in_bf[:, _cols(V_BASE, N_BLK)])), gv_ref[...])
    vs_ref[:, 0, :] = v

    phb = _dot(h, w_in_bf[:, _cols(HB_BASE, 2 * N_BLK)])
    pg = _dot(h, w_in_bf[:, _cols(G2_BASE, 2 * N_BLK)])
    s0 = st_ref[:, 0, :]
    s1 = st_ref[:, 1, :]
    cw = cw_ref[...]
    hb_blocks, out_b_blocks = [], []
    for j in range(N_BLK):
        cs = slice(j * LANES, (j + 1) * LANES)
        x_b, gate_c = _pair(phb, j)
        gate_b, z_b = _pair(pg, j)
        hb = gate_c * x_b
        conv = s0[:, cs] * cw[0:1, cs] + s1[:, cs] * cw[1:2, cs] + hb * cw[2:3, cs]
        hb_blocks.append(hb)
        out_b_blocks.append((gate_b * conv * _silu(z_b)).astype(jnp.bfloat16))
    css_ref[:, 0, :] = s1
    css_ref[:, 1, :] = jnp.concatenate(hb_blocks, axis=-1)
    y = x + _dot(jnp.concatenate(out_b_blocks, axis=-1), w_out_bf[D_A:D_MIX, :])

    pua = _dot(h, w_in_bf[:, _cols(UA_BASE, 2 * N_BLK)])
    out_a_blocks = []
    for j in range(H_A):
        cs = slice(j * HD_A, (j + 1) * HD_A)
        u, z_a = _pair(pua, j)
        mixed = v[:, cs] * ws_ref[j, 0:1, 0:1] + b_ref[j:j + 1, 0:1]
        out_a_blocks.append((_gelu(u) * mixed * _silu(z_a)).astype(jnp.bfloat16))
    y = y + _dot(jnp.concatenate(out_a_blocks, axis=-1), w_out_bf[0:D_A, :])
    ys_ref[:, 0, :] = _rmsnorm(y, gf_ref[...])


def _prompt_step(t, x_ref, gn_ref, ws_ref, b_ref, gv_ref, cw_ref, gf_ref,
                 w_in_bf, w_out_bf, y_ref, cs_ref, ext_ref, mixed_ref):
    tm = x_ref.shape[0]
    x = x_ref[...]
    h = _rmsnorm(x, gn_ref[...]).astype(jnp.bfloat16)

    v = _rmsnorm(_gelu(_dot(h, w_in_bf[:, _cols(V_BASE, N_BLK)])), gv_ref[...])
    v_bf = v.astype(jnp.bfloat16)

    phb = _dot(h, w_in_bf[:, _cols(HB_BASE, 2 * N_BLK)])
    ext_ref[0:SUBLANES, :] = jnp.where(t == 0, 0.0, ext_ref[tm:tm + SUBLANES, :])
    for j in range(N_BLK):
        x_b, gate_c = _pair(phb, j)
        ext_ref[SUBLANES:SUBLANES + tm, j * LANES:(j + 1) * LANES] = gate_c * x_b
    cs_ref[...] = ext_ref[SUBLANES + tm - (CONV_W - 1):SUBLANES + tm, :]

    pg = _dot(h, w_in_bf[:, _cols(G2_BASE, 2 * N_BLK)])
    cw = cw_ref[...]
    out_b_blocks = []
    for j in range(N_BLK):
        cs = slice(j * LANES, (j + 1) * LANES)
        gate_b, z_b = _pair(pg, j)
        conv = (ext_ref[SUBLANES - 2:SUBLANES - 2 + tm, cs] * cw[0:1, cs]
                + ext_ref[SUBLANES - 1:SUBLANES - 1 + tm, cs] * cw[1:2, cs]
                + ext_ref[SUBLANES:SUBLANES + tm, cs] * cw[2:3, cs])
        out_b_blocks.append((gate_b * conv * _silu(z_b)).astype(jnp.bfloat16))
    y = x + _dot(jnp.concatenate(out_b_blocks, axis=-1), w_out_bf[D_A:D_MIX, :])

    row = lax.broadcasted_iota(jnp.int32, (CHUNK, CHUNK), 0)
    col = lax.broadcasted_iota(jnp.int32, (CHUNK, CHUNK), 1)
    bt = jnp.transpose(b_ref[...])
    for hd in range(H_A):
        w_tril = jnp.where(row >= col, ws_ref[hd], 0.0).astype(jnp.bfloat16)
        cs = slice(hd * HD_A, (hd + 1) * HD_A)
        for c in range(tm // CHUNK):
            rs = slice(c * CHUNK, (c + 1) * CHUNK)
            mixed_ref[rs, cs] = _dot(w_tril, v_bf[rs, cs]) + bt[:, hd:hd + 1]

    pua = _dot(h, w_in_bf[:, _cols(UA_BASE, 2 * N_BLK)])
    out_a_blocks = []
    for j in range(H_A):
        u, z_a = _pair(pua, j)
        out_a_blocks.append((_gelu(u) * mixed_ref[:, j * HD_A:(j + 1) * HD_A]
                             * _silu(z_a)).astype(jnp.bfloat16))

    y = y + _dot(jnp.concatenate(out_a_blocks, axis=-1), w_out_bf[0:D_A, :])
    y_ref[...] = _rmsnorm(y, gf_ref[...])


def _layer_kernel(tiles_per_seq,
                  x_ref, xs_ref, st_ref, gn_ref, w_in_hbm, ws_ref, b_ref, gv_ref, cw_ref,
                  w_out_hbm, gf_ref,
                  y_ref, cs_ref, ys_ref, css_ref, vs_ref,
                  w_in_bf, w_out_bf, stage, sem, ext_ref, mixed_ref):
    s = pl.program_id(0)
    small = (gn_ref, ws_ref, b_ref, gv_ref, cw_ref, gf_ref)

    @pl.when(s == 0)
    def _():
        _load_weights_bf16(w_in_hbm, w_out_hbm, w_in_bf, w_out_bf, stage, sem)
        _sample_step(xs_ref, st_ref, *small, w_in_bf, w_out_bf, ys_ref, css_ref, vs_ref)

    @pl.when(s > 0)
    def _():
        t = lax.rem(s - 1, tiles_per_seq)
        _prompt_step(t, x_ref, *small, w_in_bf, w_out_bf, y_ref, cs_ref, ext_ref, mixed_ref)


def _resident(shape, squeeze_leading=False):
    block = ((None,) + tuple(shape[1:])) if squeeze_leading else tuple(shape)
    return pl.BlockSpec(block, lambda s: (0,) * len(shape), pipeline_mode=pl.Buffered(1))


def _whole_out(shape, squeeze_leading=False):
    block = ((None,) + tuple(shape[1:])) if squeeze_leading else tuple(shape)
    return pl.BlockSpec(block, lambda s: (0,) * len(shape))


def kernel(x_prompt, x_sample, state_conv, g_norm, w_in, w_s, b_s, g_v, conv_w, w_out, g_final):
    depth = w_in.shape[0]
    assert depth == 1, "final norm is fused into the single layer"
    batch, seq, _ = x_prompt.shape
    n_s, dec_seq, _ = x_sample.shape
    assert dec_seq == 1 and n_s <= TILE_M
    assert seq % TILE_M == 0 and TILE_M % CHUNK == 0
    assert D_MODEL % STAGE_ROWS == 0 and D_MIX % STAGE_ROWS == 0
    assert D_IN % STAGE_COLS == 0 and D_MODEL % STAGE_COLS == 0 and STAGE_COLS % LANES == 0
    tiles_per_seq = seq // TILE_M
    n_tiles = batch * tiles_per_seq

    def tile_index(s):
        i = jnp.maximum(s - 1, 0)
        return i // tiles_per_seq, i % tiles_per_seq

    def x_map(s):
        b, t = tile_index(s)
        return b, t, 0

    def cs_map(s):
        b, _ = tile_index(s)
        return b, 0, 0

    cs_s_shape = (depth, n_s, CONV_W - 1, D_B)
    v_s_shape = (depth, n_s, 1, D_A)
    y_p, cs_p, y_s, cs_s, v_s = pl.pallas_call(
        lambda *refs: _layer_kernel(tiles_per_seq, *refs),
        grid=(n_tiles + 1,),
        in_specs=[
            pl.BlockSpec((None, TILE_M, D_MODEL), x_map),
            _resident(x_sample.shape),
            _resident(state_conv.shape, squeeze_leading=True),
            _resident((1, D_MODEL)),
            pl.BlockSpec(memory_space=pl.ANY),
            _resident(w_s.shape, squeeze_leading=True),
            _resident(b_s.shape, squeeze_leading=True),
            _resident((1, D_A)),
            _resident(conv_w.shape, squeeze_leading=True),
            pl.BlockSpec(memory_space=pl.ANY),
            _resident((1, D_MODEL)),
        ],
        out_specs=[
            pl.BlockSpec((None, TILE_M, D_MODEL), x_map),
            pl.BlockSpec((None, CONV_W - 1, D_B), cs_map),
            _whole_out(x_sample.shape),
            _whole_out(cs_s_shape, squeeze_leading=True),
            _whole_out(v_s_shape, squeeze_leading=True),
        ],
        out_shape=[
            jax.ShapeDtypeStruct((batch, seq, D_MODEL), jnp.float32),
            jax.ShapeDtypeStruct((batch, CONV_W - 1, D_B), jnp.float32),
            jax.ShapeDtypeStruct(x_sample.shape, jnp.float32),
            jax.ShapeDtypeStruct(cs_s_shape, jnp.float32),
            jax.ShapeDtypeStruct(v_s_shape, jnp.float32),
        ],
        scratch_shapes=[
            pltpu.VMEM((D_MODEL, D_IN), jnp.bfloat16),
            pltpu.VMEM((D_MIX, D_MODEL), jnp.bfloat16),
            pltpu.VMEM((2, STAGE_ROWS, STAGE_COLS), jnp.float32),
            pltpu.SemaphoreType.DMA((2,)),
            pltpu.VMEM((SUBLANES + TILE_M, D_B), jnp.float32),
            pltpu.VMEM((TILE_M, D_A), jnp.float32),
        ],
        compiler_params=pltpu.CompilerParams(
            dimension_semantics=("arbitrary",),
            vmem_limit_bytes=VMEM_LIMIT_BYTES),
        name="hybrid_layer",
    )(x_prompt, x_sample, state_conv, g_norm, w_in[0], w_s, b_s, g_v, conv_w, w_out[0],
      g_final[None, :])
    return y_p, y_s, cs_p[None], cs_s, v_s
```

```python
import jax
import jax.numpy as jnp
import numpy as np
from jax import lax
from jax.experimental import pallas as pl
from jax.experimental.pallas import tpu as pltpu

D_MODEL = 1024
D_A = 1024
D_B = 1024
D_MIX = D_A + D_B
D_IN = 3 * D_A + 4 * D_B
CHUNK = 128
HD_A = 128
H_A = D_A // HD_A
CONV_W = 3
EPS = 1e-5
SQRT_HALF = float(np.sqrt(0.5))

OFF_U, OFF_V, OFF_ZA, OFF_XB, OFF_GB, OFF_GC, OFF_ZB = (
    0, D_A, 2 * D_A, 3 * D_A, 3 * D_A + D_B, 3 * D_A + 2 * D_B, 3 * D_A + 3 * D_B)

SUBLANES = 8
LANES = 128
TILE_M = 1024
SUB_M = 512
STAGE_ROWS, STAGE_COLS = 1024, 512
VMEM_LIMIT_BYTES = 58 * 1024 * 1024

N_BLK = D_A // LANES
V_BASE, HB_BASE, G2_BASE, UA_BASE = 0, N_BLK, 3 * N_BLK, 5 * N_BLK


def _dst_block(src_block):
    group, j = divmod(src_block, N_BLK)
    base, second = {OFF_V: (V_BASE, None), OFF_XB: (HB_BASE, 0), OFF_GC: (HB_BASE, 1),
                    OFF_GB: (G2_BASE, 0), OFF_ZB: (G2_BASE, 1),
                    OFF_U: (UA_BASE, 0), OFF_ZA: (UA_BASE, 1)}[group * D_A]
    return base + j if second is None else base + 2 * j + second


def _cols(base, n_blocks):
    return slice(base * LANES, (base + n_blocks) * LANES)


def _pair(p, j):
    return (p[:, (2 * j) * LANES:(2 * j + 1) * LANES],
            p[:, (2 * j + 1) * LANES:(2 * j + 2) * LANES])


def _dot(a, b):
    return jnp.dot(a, b, preferred_element_type=jnp.float32)


def _rmsnorm(x, g):
    inv = lax.rsqrt(jnp.mean(x * x, axis=-1, keepdims=True) + EPS)
    return x * inv * g


def _gelu(x):
    return 0.5 * x * (1.0 + lax.erf(x * SQRT_HALF))


def _silu(x):
    return x * jax.nn.sigmoid(x)


def _load_weights_bf16(w_in_hbm, w_out_hbm, w_in_bf, w_out_bf, stage, sem):
    def blocks(src):
        rows, cols = src.shape
        return [(src, r, c) for r in range(0, rows, STAGE_ROWS)
                for c in range(0, cols, STAGE_COLS)]

    chunks = blocks(w_in_hbm) + blocks(w_out_hbm)

    def copy(k):
        src, r, c = chunks[k]
        return pltpu.make_async_copy(src.at[pl.ds(r, STAGE_ROWS), pl.ds(c, STAGE_COLS)],
                                     stage.at[k % 2], sem.at[k % 2])

    copy(0).start()
    for k in range(len(chunks)):
        if k + 1 < len(chunks):
            copy(k + 1).start()
        copy(k).wait()
        src, r, c = chunks[k]
        if src is w_in_hbm:
            for i in range(STAGE_COLS // LANES):
                dst = _dst_block(c // LANES + i)
                w_in_bf[:, dst * LANES:(dst + 1) * LANES] = (
                    stage[k % 2, :, i * LANES:(i + 1) * LANES].astype(jnp.bfloat16))
        else:
            w_out_bf[r:r + STAGE_ROWS, c:c + STAGE_COLS] = stage[k % 2].astype(jnp.bfloat16)


def _sample_step(xs_ref, st_ref, gn_ref, ws_ref, b_ref, gv_ref, cw_ref, gf_ref,
                 w_in_bf, w_out_bf, ys_ref, css_ref, vs_ref):
    x = xs_ref[:, 0, :]
    h = _rmsnorm(x, gn_ref[...]).astype(jnp.bfloat16)

    v = _rmsnorm(_gelu(_dot(h, w_in_bf[:, _cols(V_BASE, N_BLK)])), gv_ref[...])
    vs_ref[:, 0, :] = v

    phb = _dot(h, w_in_bf[:, _cols(HB_BASE, 2 * N_BLK)])
    pg = _dot(h, w_in_bf[:, _cols(G2_BASE, 2 * N_BLK)])
    s0 = st_ref[:, 0, :]
    s1 = st_ref[:, 1, :]
    cw = cw_ref[...]
    hb_blocks, out_b_blocks = [], []
    for j in range(N_BLK):
        cs = slice(j * LANES, (j + 1) * LANES)
        x_b, gate_c = _pair(phb, j)
        gate_b, z_b = _pair(pg, j)
        hb = gate_c * x_b
        conv = s0[:, cs] * cw[0:1, cs] + s1[:, cs] * cw[1:2, cs] + hb * cw[2:3, cs]
        hb_blocks.append(hb)
        out_b_blocks.append((gate_b * conv * _silu(z_b)).astype(jnp.bfloat16))
    css_ref[:, 0, :] = s1
    css_ref[:, 1, :] = jnp.concatenate(hb_blocks, axis=-1)
    y = x + _dot(jnp.concatenate(out_b_blocks, axis=-1), w_out_bf[D_A:D_MIX, :])

    pua = _dot(h, w_in_bf[:, _cols(UA_BASE, 2 * N_BLK)])
    out_a_blocks = []
    for j in range(H_A):
        cs = slice(j * HD_A, (j + 1) * HD_A)
        u, z_a = _pair(pua, j)
        mixed = v[:, cs] * ws_ref[j, 0:1, 0:1] + b_ref[j:j + 1, 0:1]
        out_a_blocks.append((_gelu(u) * mixed * _silu(z_a)).astype(jnp.bfloat16))
    y = y + _dot(jnp.concatenate(out_a_blocks, axis=-1), w_out_bf[0:D_A, :])
    ys_ref[:, 0, :] = _rmsnorm(y, gf_ref[...])


def _prompt_step(t, x_ref, gn_ref, ws_ref, b_ref, gv_ref, cw_ref, gf_ref,
                 w_in_bf, w_out_bf, y_ref, cs_ref, ext_ref, mixed_ref):
    tm = x_ref.shape[0]
    ext_ref[0:SUBLANES, :] = jnp.where(t == 0, 0.0, ext_ref[tm:tm + SUBLANES, :])
    for r0 in range(0, tm, SUB_M):
        _prompt_rows(r0, x_ref, gn_ref, ws_ref, b_ref, gv_ref, cw_ref, gf_ref,
                     w_in_bf, w_out_bf, y_ref, ext_ref, mixed_ref)
    cs_ref[...] = ext_ref[SUBLANES + tm - (CONV_W - 1):SUBLANES + tm, :]


def _prompt_rows(r0, x_ref, gn_ref, ws_ref, b_ref, gv_ref, cw_ref, gf_ref,
                 w_in_bf, w_out_bf, y_ref, ext_ref, mixed_ref):
    rows = slice(r0, r0 + SUB_M)
    x = x_ref[rows, :]
    h = _rmsnorm(x, gn_ref[...]).astype(jnp.bfloat16)

    v = _rmsnorm(_gelu(_dot(h, w_in_bf[:, _cols(V_BASE, N_BLK)])), gv_ref[...])
    v_bf = v.astype(jnp.bfloat16)

    phb = _dot(h, w_in_bf[:, _cols(HB_BASE, 2 * N_BLK)])
    for j in range(N_BLK):
        x_b, gate_c = _pair(phb, j)
        ext_ref[SUBLANES + r0:SUBLANES + r0 + SUB_M, j * LANES:(j + 1) * LANES] = gate_c * x_b

    pg = _dot(h, w_in_bf[:, _cols(G2_BASE, 2 * N_BLK)])
    cw = cw_ref[...]
    out_b_blocks = []
    for j in range(N_BLK):
        cs = slice(j * LANES, (j + 1) * LANES)
        gate_b, z_b = _pair(pg, j)
        tap = lambda k: ext_ref[SUBLANES - (CONV_W - 1) + k + r0:
                                SUBLANES - (CONV_W - 1) + k + r0 + SUB_M, cs] * cw[k:k + 1, cs]
        conv = tap(0) + tap(1) + tap(2)
        out_b_blocks.append((gate_b * conv * _silu(z_b)).astype(jnp.bfloat16))
    y = x + _dot(jnp.concatenate(out_b_blocks, axis=-1), w_out_bf[D_A:D_MIX, :])

    row = lax.broadcasted_iota(jnp.int32, (CHUNK, CHUNK), 0)
    col = lax.broadcasted_iota(jnp.int32, (CHUNK, CHUNK), 1)
    bt = jnp.transpose(b_ref[...])
    for hd in range(H_A):
        w_tril = jnp.where(row >= col, ws_ref[hd], 0.0).astype(jnp.bfloat16)
        cs = slice(hd * HD_A, (hd + 1) * HD_A)
        for c in range(SUB_M // CHUNK):
            rs = slice(c * CHUNK, (c + 1) * CHUNK)
            mixed_ref[r0 + c * CHUNK:r0 + (c + 1) * CHUNK, cs] = (
                _dot(w_tril, v_bf[rs, cs]) + bt[:, hd:hd + 1])

    pua = _dot(h, w_in_bf[:, _cols(UA_BASE, 2 * N_BLK)])
    out_a_blocks = []
    for j in range(H_A):
        u, z_a = _pair(pua, j)
        out_a_blocks.append((_gelu(u) * mixed_ref[rows, j * HD_A:(j + 1) * HD_A]
                             * _silu(z_a)).astype(jnp.bfloat16))

    y = y + _dot(jnp.concatenate(out_a_blocks, axis=-1), w_out_bf[0:D_A, :])
    y_ref[rows, :] = _rmsnorm(y, gf_ref[...])


def _layer_kernel(tiles_per_seq,
                  x_ref, xs_ref, st_ref, gn_ref, w_in_hbm, ws_ref, b_ref, gv_ref, cw_ref,
                  w_out_hbm, gf_ref,
                  y_ref, cs_ref, ys_ref, css_ref, vs_ref,
                  w_in_bf, w_out_bf, stage, sem, ext_ref, mixed_ref):
    s = pl.program_id(0)
    small = (gn_ref, ws_ref, b_ref, gv_ref, cw_ref, gf_ref)

    @pl.when(s == 0)
    def _():
        _load_weights_bf16(w_in_hbm, w_out_hbm, w_in_bf, w_out_bf, stage, sem)
        _sample_step(xs_ref, st_ref, *small, w_in_bf, w_out_bf, ys_ref, css_ref, vs_ref)

    @pl.when(s > 0)
    def _():
        t = lax.rem(s - 1, tiles_per_seq)
        _prompt_step(t, x_ref, *small, w_in_bf, w_out_bf, y_ref, cs_ref, ext_ref, mixed_ref)


def _resident(shape, squeeze_leading=False):
    block = ((None,) + tuple(shape[1:])) if squeeze_leading else tuple(shape)
    return pl.BlockSpec(block, lambda s: (0,) * len(shape), pipeline_mode=pl.Buffered(1))


def _whole_out(shape, squeeze_leading=False):
    block = ((None,) + tuple(shape[1:])) if squeeze_leading else tuple(shape)
    return pl.BlockSpec(block, lambda s: (0,) * len(shape))


def kernel(x_prompt, x_sample, state_conv, g_norm, w_in, w_s, b_s, g_v, conv_w, w_out, g_final):
    depth = w_in.shape[0]
    assert depth == 1, "final norm is fused into the single layer"
    batch, seq, _ = x_prompt.shape
    n_s, dec_seq, _ = x_sample.shape
    assert dec_seq == 1 and n_s <= TILE_M
    assert seq % TILE_M == 0 and TILE_M % SUB_M == 0 and SUB_M % CHUNK == 0
    assert D_MODEL % STAGE_ROWS == 0 and D_MIX % STAGE_ROWS == 0
    assert D_IN % STAGE_COLS == 0 and D_MODEL % STAGE_COLS == 0 and STAGE_COLS % LANES == 0
    tiles_per_seq = seq // TILE_M
    n_tiles = batch * tiles_per_seq

    def tile_index(s):
        i = jnp.maximum(s - 1, 0)
        return i // tiles_per_seq, i % tiles_per_seq

    def x_map(s):
        b, t = tile_index(s)
        return b, t, 0

    def cs_map(s):
        b, _ = tile_index(s)
        return b, 0, 0

    cs_s_shape = (depth, n_s, CONV_W - 1, D_B)
    v_s_shape = (depth, n_s, 1, D_A)
    y_p, cs_p, y_s, cs_s, v_s = pl.pallas_call(
        lambda *refs: _layer_kernel(tiles_per_seq, *refs),
        grid=(n_tiles + 1,),
        in_specs=[
            pl.BlockSpec((None, TILE_M, D_MODEL), x_map),
            _resident(x_sample.shape),
            _resident(state_conv.shape, squeeze_leading=True),
            _resident((1, D_MODEL)),
            pl.BlockSpec(memory_space=pl.ANY),
            _resident(w_s.shape, squeeze_leading=True),
            _resident(b_s.shape, squeeze_leading=True),
            _resident((1, D_A)),
            _resident(conv_w.shape, squeeze_leading=True),
            pl.BlockSpec(memory_space=pl.ANY),
            _resident((1, D_MODEL)),
        ],
        out_specs=[
            pl.BlockSpec((None, TILE_M, D_MODEL), x_map),
            pl.BlockSpec((None, CONV_W - 1, D_B), cs_map),
            _whole_out(x_sample.shape),
            _whole_out(cs_s_shape, squeeze_leading=True),
            _whole_out(v_s_shape, squeeze_leading=True),
        ],
        out_shape=[
            jax.ShapeDtypeStruct((batch, seq, D_MODEL), jnp.float32),
            jax.ShapeDtypeStruct((batch, CONV_W - 1, D_B), jnp.float32),
            jax.ShapeDtypeStruct(x_sample.shape, jnp.float32),
            jax.ShapeDtypeStruct(cs_s_shape, jnp.float32),
            jax.ShapeDtypeStruct(v_s_shape, jnp.float32),
        ],
        scratch_shapes=[
            pltpu.VMEM((D_MODEL, D_IN), jnp.bfloat16),
            pltpu.VMEM((D_MIX, D_MODEL), jnp.bfloat16),
            pltpu.VMEM((2, STAGE_ROWS, STAGE_COLS), jnp.float32),
            pltpu.SemaphoreType.DMA((2,)),
            pltpu.VMEM((SUBLANES + TILE_M, D_B), jnp.float32),
            pltpu.VMEM((TILE_M, D_A), jnp.float32),
        ],
        compiler_params=pltpu.CompilerParams(
            dimension_semantics=("arbitrary",),
            vmem_limit_bytes=VMEM_LIMIT_BYTES),
        name="hybrid_layer",
    )(x_prompt, x_sample, state_conv, g_norm, w_in[0], w_s, b_s, g_v, conv_w, w_out[0],
      g_final[None, :])
    return y_p, y_s, cs_p[None], cs_s, v_s
```

```python
import jax
import jax.numpy as jnp
import numpy as np
from jax import lax
from jax.experimental import pallas as pl
from jax.experimental.pallas import tpu as pltpu

D_MODEL = 1024
D_A = 1024
D_B = 1024
D_MIX = D_A + D_B
D_IN = 3 * D_A + 4 * D_B
CHUNK = 128
HD_A = 128
H_A = D_A // HD_A
CONV_W = 3
EPS = 1e-5
SQRT_HALF = float(np.sqrt(0.5))

OFF_U, OFF_V, OFF_ZA, OFF_XB, OFF_GB, OFF_GC, OFF_ZB = (
    0, D_A, 2 * D_A, 3 * D_A, 3 * D_A + D_B, 3 * D_A + 2 * D_B, 3 * D_A + 3 * D_B)

SUBLANES = 8
LANES = 128
TILE_M = 1024
SUB_M = 512
STAGE_ROWS, STAGE_COLS = 1024, 256
N_STAGE = 4
VMEM_LIMIT_BYTES = 58 * 1024 * 1024

N_BLK = D_A // LANES
V_BASE, HB_BASE, G2_BASE, UA_BASE = 0, N_BLK, 3 * N_BLK, 5 * N_BLK


def _dst_block(src_block):
    group, j = divmod(src_block, N_BLK)
    base, second = {OFF_V: (V_BASE, None), OFF_XB: (HB_BASE, 0), OFF_GC: (HB_BASE, 1),
                    OFF_GB: (G2_BASE, 0), OFF_ZB: (G2_BASE, 1),
                    OFF_U: (UA_BASE, 0), OFF_ZA: (UA_BASE, 1)}[group * D_A]
    return base + j if second is None else base + 2 * j + second


def _cols(base, n_blocks):
    return slice(base * LANES, (base + n_blocks) * LANES)


def _pair(p, j):
    return (p[:, (2 * j) * LANES:(2 * j + 1) * LANES],
            p[:, (2 * j + 1) * LANES:(2 * j + 2) * LANES])


def _dot(a, b):
    return jnp.dot(a, b, preferred_element_type=jnp.float32)


def _rmsnorm(x, g):
    inv = lax.rsqrt(jnp.mean(x * x, axis=-1, keepdims=True) + EPS)
    return x * inv * g


def _gelu(x):
    return 0.5 * x * (1.0 + lax.erf(x * SQRT_HALF))


def _silu(x):
    return x * jax.nn.sigmoid(x)


class _WeightStream:
    def __init__(self, w_in_hbm, w_out_hbm, w_in_bf, w_out_bf, stage, sem):
        self.w_in_hbm, self.w_in_bf, self.w_out_bf = w_in_hbm, w_in_bf, w_out_bf
        self.stage, self.sem = stage, sem

        def cols(src, off, r0=0):
            return [(src, r0, c) for c in range(off, off + D_A, STAGE_COLS)]

        self.groups = {
            "v": cols(w_in_hbm, OFF_V),
            "hb": cols(w_in_hbm, OFF_XB) + cols(w_in_hbm, OFF_GC),
            "g2": cols(w_in_hbm, OFF_GB) + cols(w_in_hbm, OFF_ZB),
            "ua": cols(w_in_hbm, OFF_U) + cols(w_in_hbm, OFF_ZA),
            "out_b": cols(w_out_hbm, 0, r0=D_A),
            "out_a": cols(w_out_hbm, 0, r0=0),
        }
        self.chunks = [c for g in self.groups.values() for c in g]
        self.done = 0
        for k in range(min(N_STAGE, len(self.chunks))):
            self._copy(k).start()

    def _copy(self, k):
        src, r, c = self.chunks[k]
        slot = k % N_STAGE
        return pltpu.make_async_copy(src.at[pl.ds(r, STAGE_ROWS), pl.ds(c, STAGE_COLS)],
                                     self.stage.at[slot], self.sem.at[slot])

    def take(self, group):
        for _ in self.groups[group]:
            k = self.done
            src, r, c = self.chunks[k]
            slot = k % N_STAGE
            self._copy(k).wait()
            if src is self.w_in_hbm:
                for i in range(STAGE_COLS // LANES):
                    dst = _dst_block(c // LANES + i)
                    self.w_in_bf[:, dst * LANES:(dst + 1) * LANES] = (
                        self.stage[slot, :, i * LANES:(i + 1) * LANES].astype(jnp.bfloat16))
            else:
                self.w_out_bf[r:r + STAGE_ROWS, c:c + STAGE_COLS] = (
                    self.stage[slot].astype(jnp.bfloat16))
            if k + N_STAGE < len(self.chunks):
                self._copy(k + N_STAGE).start()
            self.done += 1


def _sample_step(weights, xs_ref, st_ref, gn_ref, ws_ref, b_ref, gv_ref, cw_ref, gf_ref,
                 w_in_bf, w_out_bf, ys_ref, css_ref, vs_ref):
    x = xs_ref[:, 0, :]
    h = _rmsnorm(x, gn_ref[...]).astype(jnp.bfloat16)

    weights.take("v")
    v = _rmsnorm(_gelu(_dot(h, w_in_bf[:, _cols(V_BASE, N_BLK)])), gv_ref[...])
    vs_ref[:, 0, :] = v

    weights.take("hb")
    phb = _dot(h, w_in_bf[:, _cols(HB_BASE, 2 * N_BLK)])
    s0 = st_ref[:, 0, :]
    s1 = st_ref[:, 1, :]
    cw = cw_ref[...]
    hb_blocks, conv_blocks = [], []
    for j in range(N_BLK):
        cs = slice(j * LANES, (j + 1) * LANES)
        x_b, gate_c = _pair(phb, j)
        hb = gate_c * x_b
        hb_blocks.append(hb)
        conv_blocks.append(s0[:, cs] * cw[0:1, cs] + s1[:, cs] * cw[1:2, cs] + hb * cw[2:3, cs])
    css_ref[:, 0, :] = s1
    css_ref[:, 1, :] = jnp.concatenate(hb_blocks, axis=-1)

    weights.take("g2")
    pg = _dot(h, w_in_bf[:, _cols(G2_BASE, 2 * N_BLK)])
    out_b_blocks = []
    for j in range(N_BLK):
        gate_b, z_b = _pair(pg, j)
        out_b_blocks.append((gate_b * conv_blocks[j] * _silu(z_b)).astype(jnp.bfloat16))

    weights.take("ua")
    pua = _dot(h, w_in_bf[:, _cols(UA_BASE, 2 * N_BLK)])
    out_a_blocks = []
    for j in range(H_A):
        cs = slice(j * HD_A, (j + 1) * HD_A)
        u, z_a = _pair(pua, j)
        mixed = v[:, cs] * ws_ref[j, 0:1, 0:1] + b_ref[j:j + 1, 0:1]
        out_a_blocks.append((_gelu(u) * mixed * _silu(z_a)).astype(jnp.bfloat16))

    weights.take("out_b")
    y = x + _dot(jnp.concatenate(out_b_blocks, axis=-1), w_out_bf[D_A:D_MIX, :])
    weights.take("out_a")
    y = y + _dot(jnp.concatenate(out_a_blocks, axis=-1), w_out_bf[0:D_A, :])
    ys_ref[:, 0, :] = _rmsnorm(y, gf_ref[...])


def _prompt_step(t, x_ref, gn_ref, ws_ref, b_ref, gv_ref, cw_ref, gf_ref,
                 w_in_bf, w_out_bf, y_ref, cs_ref, ext_ref, mixed_ref):
    tm = x_ref.shape[0]
    ext_ref[0:SUBLANES, :] = jnp.where(t == 0, 0.0, ext_ref[tm:tm + SUBLANES, :])
    for r0 in range(0, tm, SUB_M):
        _prompt_rows(r0, x_ref, gn_ref, ws_ref, b_ref, gv_ref, cw_ref, gf_ref,
                     w_in_bf, w_out_bf, y_ref, ext_ref, mixed_ref)
    cs_ref[...] = ext_ref[SUBLANES + tm - (CONV_W - 1):SUBLANES + tm, :]


def _prompt_rows(r0, x_ref, gn_ref, ws_ref, b_ref, gv_ref, cw_ref, gf_ref,
                 w_in_bf, w_out_bf, y_ref, ext_ref, mixed_ref):
    rows = slice(r0, r0 + SUB_M)
    x = x_ref[rows, :]
    h = _rmsnorm(x, gn_ref[...]).astype(jnp.bfloat16)

    v = _rmsnorm(_gelu(_dot(h, w_in_bf[:, _cols(V_BASE, N_BLK)])), gv_ref[...])
    v_bf = v.astype(jnp.bfloat16)

    phb = _dot(h, w_in_bf[:, _cols(HB_BASE, 2 * N_BLK)])
    for j in range(N_BLK):
        x_b, gate_c = _pair(phb, j)
        ext_ref[SUBLANES + r0:SUBLANES + r0 + SUB_M, j * LANES:(j + 1) * LANES] = gate_c * x_b

    pg = _dot(h, w_in_bf[:, _cols(G2_BASE, 2 * N_BLK)])
    cw = cw_ref[...]
    out_b_blocks = []
    for j in range(N_BLK):
        cs = slice(j * LANES, (j + 1) * LANES)
        gate_b, z_b = _pair(pg, j)
        tap = lambda k: ext_ref[SUBLANES - (CONV_W - 1) + k + r0:
                                SUBLANES - (CONV_W - 1) + k + r0 + SUB_M, cs] * cw[k:k + 1, cs]
        conv = tap(0) + tap(1) + tap(2)
        out_b_blocks.append((gate_b * conv * _silu(z_b)).astype(jnp.bfloat16))
    y = x + _dot(jnp.concatenate(out_b_blocks, axis=-1), w_out_bf[D_A:D_MIX, :])

    row = lax.broadcasted_iota(jnp.int32, (CHUNK, CHUNK), 0)
    col = lax.broadcasted_iota(jnp.int32, (CHUNK, CHUNK), 1)
    bt = jnp.transpose(b_ref[...])
    for hd in range(H_A):
        w_tril = jnp.where(row >= col, ws_ref[hd], 0.0).astype(jnp.bfloat16)
        cs = slice(hd * HD_A, (hd + 1) * HD_A)
        for c in range(SUB_M // CHUNK):
            rs = slice(c * CHUNK, (c + 1) * CHUNK)
            mixed_ref[r0 + c * CHUNK:r0 + (c + 1) * CHUNK, cs] = (
                _dot(w_tril, v_bf[rs, cs]) + bt[:, hd:hd + 1])

    pua = _dot(h, w_in_bf[:, _cols(UA_BASE, 2 * N_BLK)])
    out_a_blocks = []
    for j in range(H_A):
        u, z_a = _pair(pua, j)
        out_a_blocks.append((_gelu(u) * mixed_ref[rows, j * HD_A:(j + 1) * HD_A]
                             * _silu(z_a)).astype(jnp.bfloat16))

    y = y + _dot(jnp.concatenate(out_a_blocks, axis=-1), w_out_bf[0:D_A, :])
    y_ref[rows, :] = _rmsnorm(y, gf_ref[...])


def _layer_kernel(tiles_per_seq,
                  x_ref, xs_ref, st_ref, gn_ref, w_in_hbm, ws_ref, b_ref, gv_ref, cw_ref,
                  w_out_hbm, gf_ref,
                  y_ref, cs_ref, ys_ref, css_ref, vs_ref,
                  w_in_bf, w_out_bf, stage, sem, ext_ref, mixed_ref):
    s = pl.program_id(0)
    small = (gn_ref, ws_ref, b_ref, gv_ref, cw_ref, gf_ref)

    @pl.when(s == 0)
    def _():
        weights = _WeightStream(w_in_hbm, w_out_hbm, w_in_bf, w_out_bf, stage, sem)
        _sample_step(weights, xs_ref, st_ref, *small, w_in_bf, w_out_bf,
                     ys_ref, css_ref, vs_ref)

    @pl.when(s > 0)
    def _():
        t = lax.rem(s - 1, tiles_per_seq)
        _prompt_step(t, x_ref, *small, w_in_bf, w_out_bf, y_ref, cs_ref, ext_ref, mixed_ref)


def _resident(shape, squeeze_leading=False):
    block = ((None,) + tuple(shape[1:])) if squeeze_leading else tuple(shape)
    return pl.BlockSpec(block, lambda s: (0,) * len(shape), pipeline_mode=pl.Buffered(1))


def _whole_out(shape, squeeze_leading=False):
    block = ((None,) + tuple(shape[1:])) if squeeze_leading else tuple(shape)
    return pl.BlockSpec(block, lambda s: (0,) * len(shape))


def kernel(x_prompt, x_sample, state_conv, g_norm, w_in, w_s, b_s, g_v, conv_w, w_out, g_final):
    depth = w_in.shape[0]
    assert depth == 1, "final norm is fused into the single layer"
    batch, seq, _ = x_prompt.shape
    n_s, dec_seq, _ = x_sample.shape
    assert dec_seq == 1 and n_s <= TILE_M
    assert seq % TILE_M == 0 and TILE_M % SUB_M == 0 and SUB_M % CHUNK == 0
    assert STAGE_ROWS == D_MODEL == D_A == D_B and D_A % STAGE_COLS == 0 and STAGE_COLS % LANES == 0
    tiles_per_seq = seq // TILE_M
    n_tiles = batch * tiles_per_seq

    def tile_index(s):
        i = jnp.maximum(s - 1, 0)
        return i // tiles_per_seq, i % tiles_per_seq

    def x_map(s):
        b, t = tile_index(s)
        return b, t, 0

    def cs_map(s):
        b, _ = tile_index(s)
        return b, 0, 0

    cs_s_shape = (depth, n_s, CONV_W - 1, D_B)
    v_s_shape = (depth, n_s, 1, D_A)
    y_p, cs_p, y_s, cs_s, v_s = pl.pallas_call(
        lambda *refs: _layer_kernel(tiles_per_seq, *refs),
        grid=(n_tiles + 1,),
        in_specs=[
            pl.BlockSpec((None, TILE_M, D_MODEL), x_map),
            _resident(x_sample.shape),
            _resident(state_conv.shape, squeeze_leading=True),
            _resident((1, D_MODEL)),
            pl.BlockSpec(memory_space=pl.ANY),
            _resident(w_s.shape, squeeze_leading=True),
            _resident(b_s.shape, squeeze_leading=True),
            _resident((1, D_A)),
            _resident(conv_w.shape, squeeze_leading=True),
            pl.BlockSpec(memory_space=pl.ANY),
            _resident((1, D_MODEL)),
        ],
        out_specs=[
            pl.BlockSpec((None, TILE_M, D_MODEL), x_map),
            pl.BlockSpec((None, CONV_W - 1, D_B), cs_map),
            _whole_out(x_sample.shape),
            _whole_out(cs_s_shape, squeeze_leading=True),
            _whole_out(v_s_shape, squeeze_leading=True),
        ],
        out_shape=[
            jax.ShapeDtypeStruct((batch, seq, D_MODEL), jnp.float32),
            jax.ShapeDtypeStruct((batch, CONV_W - 1, D_B), jnp.float32),
            jax.ShapeDtypeStruct(x_sample.shape, jnp.float32),
            jax.ShapeDtypeStruct(cs_s_shape, jnp.float32),
            jax.ShapeDtypeStruct(v_s_shape, jnp.float32),
        ],
        scratch_shapes=[
            pltpu.VMEM((D_MODEL, D_IN), jnp.bfloat16),
            pltpu.VMEM((D_MIX, D_MODEL), jnp.bfloat16),
            pltpu.VMEM((N_STAGE, STAGE_ROWS, STAGE_COLS), jnp.float32),
            pltpu.SemaphoreType.DMA((N_STAGE,)),
            pltpu.VMEM((SUBLANES + TILE_M, D_B), jnp.float32),
            pltpu.VMEM((TILE_M, D_A), jnp.float32),
        ],
        compiler_params=pltpu.CompilerParams(
            dimension_semantics=("arbitrary",),
            vmem_limit_bytes=VMEM_LIMIT_BYTES),
        name="hybrid_layer",
    )(x_prompt, x_sample, state_conv, g_norm, w_in[0], w_s, b_s, g_v, conv_w, w_out[0],
      g_final[None, :])
    return y_p, y_s, cs_p[None], cs_s, v_s
```
